```python
import math
import jax
import jax.numpy as jnp
from jax import lax
import numpy as np

D_MODEL = 1024
BATCH = 2
SEQ = 8192
DEPTH = 2

GRID_W = 64
CTX_LEN = 256
Q_BLOCK = 128
WINDOW = 128
BAND = Q_BLOCK + 2 * WINDOW
N_BRANCH = 4
N_MOD = 6
EPS = 1e-6
ROPE_THETA = 10000.0
NEG_INF = -1e30
HEAD_DIM = 64
DIFF_HEADS = 4
DIFF_QK_DIM = 64
DIFF_V_DIM = 2 * DIFF_QK_DIM
MLA_HEADS = 8
MLA_Q_LORA = 256
MLA_KV_LORA = 256
MLA_NOPE_DIM = 64
MLA_ROPE_DIM = 32
MLA_V_DIM = 64
GQA_HEADS = 8
GQA_KV_HEADS = 2
SWA_HEADS = 8
SWA_KV_HEADS = 2
BRANCH_WIDTH = 512
D_FF = 2816
SPLIT_WIDTHS = (
    DIFF_HEADS * 2 * DIFF_QK_DIM,
    DIFF_HEADS * 2 * DIFF_QK_DIM,
    DIFF_HEADS * DIFF_V_DIM,
    MLA_Q_LORA,
    MLA_KV_LORA + MLA_ROPE_DIM,
    GQA_HEADS * HEAD_DIM, GQA_KV_HEADS * HEAD_DIM, GQA_KV_HEADS * HEAD_DIM,
    SWA_HEADS * HEAD_DIM, SWA_KV_HEADS * HEAD_DIM, SWA_KV_HEADS * HEAD_DIM,
    N_BRANCH * D_MODEL,
)
IN_WIDTH = sum(SPLIT_WIDTHS)
f32 = jnp.float32

kernel_name = 'hybrid_dit_prefix_block'


def rms_norm(x, gain=None):
    xf = x.astype(f32)
    y = xf * lax.rsqrt(jnp.mean(xf * xf, axis=-1, keepdims=True) + EPS)
    if gain is not None:
        y = y * gain.astype(f32)
    return y.astype(x.dtype)


def modulate(x, shift, scale):
    return rms_norm(x) * (1.0 + scale) + shift


def axial_rope_tables(row, col, rot_dim):
    axis_dim = rot_dim // 2
    inv = ROPE_THETA ** (-jnp.arange(0, axis_dim, 2, dtype=f32) / axis_dim)
    ang = jnp.concatenate([row[:, None].astype(f32) * inv, col[:, None].astype(f32) * inv], axis=-1)
    return jnp.cos(ang), jnp.sin(ang)


def rope_2d(x, cos, sin):
    xf = x.astype(f32).reshape(*x.shape[:-1], x.shape[-1] // 2, 2)
    x0, x1 = xf[..., 0], xf[..., 1]
    c, s = cos[:, None, :], sin[:, None, :]
    return jnp.stack([x0 * c - x1 * s, x0 * s + x1 * c], axis=-1).reshape(x.shape).astype(x.dtype)


def split_columns(z):
    out, start = [], 0
    for w in SPLIT_WIDTHS:
        out.append(z[..., start:start + w])
        start += w
    return out


def to_blocks(t):
    B, N = t.shape[:2]
    return jnp.moveaxis(t.reshape(B, N // Q_BLOCK, Q_BLOCK, *t.shape[2:]), 1, 0)


def from_blocks(t):
    nb, B = t.shape[:2]
    return jnp.moveaxis(t, 0, 1).reshape(B, nb * Q_BLOCK, *t.shape[3:])


def sweep_query_blocks(fn, q):
    return from_blocks(lax.map(fn, to_blocks(q)))


def window_band(x):
    B, N = x.shape[:2]
    xp = jnp.pad(x, [(0, 0), (WINDOW, WINDOW)] + [(0, 0)] * (x.ndim - 2))
    xb = xp.reshape(B, N // Q_BLOCK + 2, Q_BLOCK, *x.shape[2:])
    return jnp.concatenate([xb[:, :-2], xb[:, 1:-1], xb[:, 2:]], axis=2)


def gqa_attend(q, k, v, scale):
    s = jnp.einsum('bqhgd,bkhd->bhgqk', q, k).astype(f32) * scale
    p = jax.nn.softmax(s, axis=-1).astype(v.dtype)
    return jnp.einsum('bhgqk,bkhd->bqhgd', p, v)


def softmax_with_sink(s, sink):
    col = jnp.broadcast_to(sink.astype(f32)[None, :, :, None, None], s.shape[:-1] + (1,))
    return jax.nn.softmax(jnp.concatenate([col, s], axis=-1), axis=-1)[..., 1:]


def diff_attend(q, k, v, lam):
    s = jnp.einsum('bqhmd,bkhmd->bhmqk', q, k).astype(f32) * (DIFF_QK_DIM ** -0.5)
    p = jax.nn.softmax(s, axis=-1)
    pd = (p[:, :, 0] - lam * p[:, :, 1]).astype(v.dtype)
    return jnp.einsum('bhqk,bkhd->bqhd', pd, v)


def diff_attention(q, k, v, qc, kc, vc, lam_vec, subln, layer_idx, rope, with_ctx_out):
    H, d, dv = DIFF_HEADS, DIFF_QK_DIM, DIFF_V_DIM
    B, N, _ = q.shape
    L = kc.shape[1]
    lam_init = 0.8 - 0.6 * math.exp(-0.3 * layer_idx)
    lv = lam_vec.astype(f32)
    lam = jnp.exp(jnp.sum(lv[0] * lv[1])) - jnp.exp(jnp.sum(lv[2] * lv[3])) + lam_init
    qh = rope_2d(q.reshape(B, N, 2 * H, d), *rope).reshape(B, N, H, 2, d)
    kh = rope_2d(k.reshape(B, N, 2 * H, d), *rope).reshape(B, N, H, 2, d)
    kch = kc.reshape(B, L, H, 2, d)
    vch = vc.reshape(B, L, H, dv)
    k_all = jnp.concatenate([kch, kh], axis=1)
    v_all = jnp.concatenate([vch, v.reshape(B, N, H, dv)], axis=1)

    def finish(o):
        return (rms_norm(o, subln) * (1.0 - lam_init)).reshape(o.shape[0], o.shape[1], H * dv)

    out = finish(sweep_query_blocks(lambda qi: diff_attend(qi, k_all, v_all, lam), qh))
    out_c = None
    if with_ctx_out:
        out_c = finish(diff_attend(qc.reshape(B, L, H, 2, d), kch, vch, lam))
    return out, out_c


def mla_attention(dq, dkv, dqc, dkvc, q_norm, kv_norm, w_uq, w_ukv, rope, with_ctx_out):
    H, dn, dr, dv = MLA_HEADS, MLA_NOPE_DIM, MLA_ROPE_DIM, MLA_V_DIM
    scale = (dn + dr) ** -0.5

    def queries(t, rope):
        B, T, _ = t.shape
        q = (rms_norm(t, q_norm) @ w_uq).reshape(B, T, H, dn + dr)
        if rope is not None:
            q = jnp.concatenate([q[..., :dn], rope_2d(q[..., dn:], *rope)], axis=-1)
        return q[:, :, :, None, :]

    def keys_values(t, rope):
        B, T, _ = t.shape
        c_kv, k_rot = t[..., :MLA_KV_LORA], t[..., MLA_KV_LORA:]
        kv = (rms_norm(c_kv, kv_norm) @ w_ukv).reshape(B, T, H, dn + dv)
        k_rot = k_rot[:, :, None, :]
        if rope is not None:
            k_rot = rope_2d(k_rot, *rope)
        k = jnp.concatenate([kv[..., :dn], jnp.broadcast_to(k_rot, (B, T, H, dr))], axis=-1)
        return k, kv[..., dn:]

    B, N, _ = dq.shape
    L = dkvc.shape[1]
    k, v = keys_values(dkv, rope)
    kc, vc = keys_values(dkvc, None)
    k_all = jnp.concatenate([kc, k], axis=1)
    v_all = jnp.concatenate([vc, v], axis=1)
    out = sweep_query_blocks(lambda qi: gqa_attend(qi, k_all, v_all, scale), queries(dq, rope)).reshape(B, N, H * dv)
    out_c = None
    if with_ctx_out:
        out_c = gqa_attend(queries(dqc, None), kc, vc, scale).reshape(B, L, H * dv)
    return out, out_c


def grid_attention(q, k, v, qc, kc, vc, q_gain, k_gain, rope, with_ctx_out):
    H, Hkv, d = GQA_HEADS, GQA_KV_HEADS, HEAD_DIM
    G = H // Hkv
    scale = d ** -0.5
    B, N, _ = q.shape
    L = kc.shape[1]
    qh = rope_2d(rms_norm(q.reshape(B, N, H, d), q_gain), *rope).reshape(B, N, Hkv, G, d)
    kh = rope_2d(rms_norm(k.reshape(B, N, Hkv, d), k_gain), *rope)
    kch = rms_norm(kc.reshape(B, L, Hkv, d), k_gain)
    vch = vc.reshape(B, L, Hkv, d)
    k_all = jnp.concatenate([kch, kh], axis=1)
    v_all = jnp.concatenate([vch, v.reshape(B, N, Hkv, d)], axis=1)
    out = sweep_query_blocks(lambda qi: gqa_attend(qi, k_all, v_all, scale), qh).reshape(B, N, H * d)
    out_c = None
    if with_ctx_out:
        qch = rms_norm(qc.reshape(B, L, H, d), q_gain).reshape(B, L, Hkv, G, d)
        out_c = gqa_attend(qch, kch, vch, scale).reshape(B, L, H * d)
    return out, out_c


def window_attention(q, k, v, qc, kc, vc, sink, rope, with_ctx_out):
    H, Hkv, d = SWA_HEADS, SWA_KV_HEADS, HEAD_DIM
    G = H // Hkv
    scale = d ** -0.5
    B, N, _ = q.shape
    L = kc.shape[1]
    sink = sink.reshape(Hkv, G)
    qh = rope_2d(q.reshape(B, N, H, d), *rope).reshape(B, N, Hkv, G, d)
    kh = rope_2d(k.reshape(B, N, Hkv, d), *rope)
    kch = kc.reshape(B, L, Hkv, d)
    vch = vc.reshape(B, L, Hkv, d)
    kb = jnp.moveaxis(window_band(kh), 1, 0)
    vb = jnp.moveaxis(window_band(v.reshape(B, N, Hkv, d)), 1, 0)
    i_idx = jnp.arange(Q_BLOCK)[:, None]
    j_idx = jnp.arange(BAND)[None, :]

    def block(args):
        qi, ki, vi, n = args
        s_ctx = jnp.einsum('bqhgd,bkhd->bhgqk', qi, kch).astype(f32) * scale
        s_band = jnp.einsum('bqhgd,bkhd->bhgqk', qi, ki).astype(f32) * scale
        kpos = n * Q_BLOCK - WINDOW + j_idx
        valid = (jnp.abs(j_idx - WINDOW - i_idx) <= WINDOW) & (kpos >= 0) & (kpos < N)
        s_band = jnp.where(valid, s_band, NEG_INF)
        p = softmax_with_sink(jnp.concatenate([s_ctx, s_band], axis=-1), sink).astype(vi.dtype)
        return (jnp.einsum('bhgqk,bkhd->bqhgd', p[..., :L], vch)
                + jnp.einsum('bhgqk,bkhd->bqhgd', p[..., L:], vi))

    out = from_blocks(lax.map(block, (to_blocks(qh), kb, vb, jnp.arange(N // Q_BLOCK)))).reshape(B, N, H * d)
    out_c = None
    if with_ctx_out:
        s = jnp.einsum('bqhgd,bkhd->bhgqk', qc.reshape(B, L, Hkv, G, d), kch).astype(f32) * scale
        p = softmax_with_sink(s, sink).astype(vch.dtype)
        out_c = jnp.einsum('bhgqk,bkhd->bqhgd', p, vch).reshape(B, L, H * d)
    return out, out_c


def merge_branches(outs, g, w_branch, w_out):
    B, T, _ = g.shape
    y = jnp.einsum('btke,ked->btkd', jnp.stack(outs, axis=2), w_branch)
    gate = jax.nn.sigmoid(g.reshape(B, T, N_BRANCH, D_MODEL))
    return jnp.sum(gate * y, axis=2) @ w_out


def token_mixing(h, hc, lp, layer_idx, rope64, rope32, with_ctx_out):
    aq, ak, av, bq, bkv, cq, ck, cv, dq, dk, dv, g = split_columns(h @ lp['w_in'])
    aqc, akc, avc, bqc, bkvc, cqc, ckc, cvc, dqc, dkc, dvc, gc = split_columns(hc @ lp['w_in'])
    oa, oa_c = diff_attention(aq, ak, av, aqc, akc, avc, lp['diff_lambda'], lp['diff_subln'], layer_idx, rope64, with_ctx_out)
    ob, ob_c = mla_attention(bq, bkv, bqc, bkvc, lp['mla_q_norm'], lp['mla_kv_norm'], lp['mla_w_uq'], lp['mla_w_ukv'], rope32, with_ctx_out)
    oc, oc_c = grid_attention(cq, ck, cv, cqc, ckc, cvc, lp['gqa_q_norm'], lp['gqa_k_norm'], rope64, with_ctx_out)
    od, od_c = window_attention(dq, dk, dv, dqc, dkc, dvc, lp['swa_sink'], rope64, with_ctx_out)
    out = merge_branches([oa, ob, oc, od], g, lp['w_branch'], lp['w_out'])
    out_c = None
    if with_ctx_out:
        out_c = merge_branches([oa_c, ob_c, oc_c, od_c], gc, lp['w_branch'], lp['w_out'])
    return out, out_c


def conv_ffn(h, lp):
    u = h @ lp['ffn_w_up']
    T = u.shape[1]
    w = lp['ffn_conv_w']
    up = jnp.pad(u, ((0, 0), (1, 1), (0, 0)))
    u = up[:, :T] * w[0] + up[:, 1:T + 1] * w[1] + up[:, 2:] * w[2] + lp['ffn_conv_b']
    val, gate = jnp.split(u, 2, axis=-1)
    return (jax.nn.silu(gate) * val) @ lp['ffn_w_down']


def hybrid_layer(x, xc, c, c_ctx, lp, layer_idx, rope64, rope32, update_ctx):
    B = x.shape[0]
    mod = (jax.nn.silu(c) @ lp['w_mod'] + lp['b_mod']).reshape(B, 1, N_MOD, D_MODEL)
    mod_c = (jax.nn.silu(c_ctx) @ lp['w_mod'] + lp['b_mod']).reshape(1, 1, N_MOD, D_MODEL)
    h = modulate(x, mod[:, :, 0], mod[:, :, 1])
    hc = modulate(xc, mod_c[:, :, 0], mod_c[:, :, 1])
    mix, mix_c = token_mixing(h, hc, lp, layer_idx, rope64, rope32, update_ctx)
    x = x + mod[:, :, 2] * mix
    x = x + mod[:, :, 5] * conv_ffn(modulate(x, mod[:, :, 3], mod[:, :, 4]), lp)
    if update_ctx:
        xc = xc + mod_c[:, :, 2] * mix_c
        xc = xc + mod_c[:, :, 5] * conv_ffn(modulate(xc, mod_c[:, :, 3], mod_c[:, :, 4]), lp)
    return x, xc


def setup_inputs(seed: int = 0) -> dict:
    key = jax.random.key(seed)
    ks = jax.random.split(key, 23)

    def nrm(i, shape, scale):
        return scale * jax.random.normal(ks[i], shape, jnp.float32)

    def gain(i, shape):
        return 1.0 + nrm(i, shape, 0.02)

    return {
        'x': nrm(0, (BATCH, SEQ, D_MODEL), 1.0),
        'c': nrm(1, (BATCH, D_MODEL), 1.0),
        'ctx': nrm(2, (BATCH, CTX_LEN, D_MODEL), 1.0),
        'c_ctx': nrm(3, (D_MODEL,), 1.0),
        'w_mod': nrm(4, (DEPTH, D_MODEL, N_MOD * D_MODEL), 0.3 * D_MODEL ** -0.5),
        'b_mod': nrm(5, (DEPTH, N_MOD * D_MODEL), 0.02),
        'w_in': nrm(6, (DEPTH, D_MODEL, IN_WIDTH), D_MODEL ** -0.5),
        'diff_lambda': nrm(7, (DEPTH, 4, DIFF_QK_DIM), 0.1),
        'diff_subln': gain(8, (DEPTH, DIFF_V_DIM)),
        'mla_q_norm': gain(9, (DEPTH, MLA_Q_LORA)),
        'mla_kv_norm': gain(10, (DEPTH, MLA_KV_LORA)),
        'mla_w_uq': nrm(11, (DEPTH, MLA_Q_LORA, MLA_HEADS * (MLA_NOPE_DIM + MLA_ROPE_DIM)), MLA_Q_LORA ** -0.5),
        'mla_w_ukv': nrm(12, (DEPTH, MLA_KV_LORA, MLA_HEADS * (MLA_NOPE_DIM + MLA_V_DIM)), MLA_KV_LORA ** -0.5),
        'gqa_q_norm': gain(13, (DEPTH, HEAD_DIM)),
        'gqa_k_norm': gain(14, (DEPTH, HEAD_DIM)),
        'swa_sink': nrm(15, (DEPTH, SWA_HEADS), 0.5),
        'w_branch': nrm(16, (DEPTH, N_BRANCH, BRANCH_WIDTH, D_MODEL), BRANCH_WIDTH ** -0.5),
        'w_out': nrm(17, (DEPTH, D_MODEL, D_MODEL), D_MODEL ** -0.5),
        'ffn_w_up': nrm(18, (DEPTH, D_MODEL, 2 * D_FF), D_MODEL ** -0.5),
        'ffn_conv_w': nrm(19, (DEPTH, 3, 2 * D_FF), 0.5),
        'ffn_conv_b': nrm(20, (DEPTH, 2 * D_FF), 0.02),
        'ffn_w_down': nrm(21, (DEPTH, D_FF, D_MODEL), D_FF ** -0.5),
        'final_norm': gain(22, (D_MODEL,)),
    }


def reference(x, c, ctx, c_ctx, w_mod, b_mod, w_in, diff_lambda, diff_subln, mla_q_norm, mla_kv_norm,
              mla_w_uq, mla_w_ukv, gqa_q_norm, gqa_k_norm, swa_sink, w_branch, w_out, ffn_w_up,
              ffn_conv_w, ffn_conv_b, ffn_w_down, final_norm):
    N = x.shape[1]
    n_rows = N // GRID_W
    row = jnp.repeat(jnp.arange(n_rows, dtype=jnp.int32), GRID_W)
    col = jnp.tile(jnp.arange(GRID_W, dtype=jnp.int32), n_rows)
    rope64 = axial_rope_tables(row, col, HEAD_DIM)
    rope32 = axial_rope_tables(row, col, MLA_ROPE_DIM)
    xc = ctx
    for l in range(DEPTH):
        lp = dict(w_mod=w_mod[l], b_mod=b_mod[l], w_in=w_in[l], diff_lambda=diff_lambda[l],
                  diff_subln=diff_subln[l], mla_q_norm=mla_q_norm[l], mla_kv_norm=mla_kv_norm[l],
                  mla_w_uq=mla_w_uq[l], mla_w_ukv=mla_w_ukv[l], gqa_q_norm=gqa_q_norm[l],
                  gqa_k_norm=gqa_k_norm[l], swa_sink=swa_sink[l], w_branch=w_branch[l], w_out=w_out[l],
                  ffn_w_up=ffn_w_up[l], ffn_conv_w=ffn_conv_w[l], ffn_conv_b=ffn_conv_b[l],
                  ffn_w_down=ffn_w_down[l])
        x, xc = hybrid_layer(x, xc, c, c_ctx, lp, l, rope64, rope32, l < DEPTH - 1)
    return rms_norm(x, final_norm)
```

```python
import functools
import math

import jax
import jax.numpy as jnp
from jax import lax
from jax.experimental import pallas as pl
from jax.experimental.pallas import tpu as pltpu

f32 = jnp.float32
bf16 = jnp.bfloat16

LANES = 128
HALF = 64
D_MODEL = 1024
GRID_W = 64
WINDOW = 128
N_BRANCH = 4
N_MOD = 6
EPS = 1e-6
ROPE_THETA = 10000.0
NEG_INF = -1e30
DIFF_HEADS = 4
DIFF_QK_DIM = 64
MLA_HEADS = 8
MLA_Q_LORA = 256
MLA_KV_LORA = 256
MLA_NOPE_DIM = 64
MLA_ROPE_DIM = 32
MLA_V_DIM = 64
GQA_HEADS = 8
BRANCH_WIDTH = 512
D_FF = 2816
FFN_CHUNK = 256
HALO = 8
VMEM_LIMIT = 56 * 1024 * 1024

_OFF = {}
_o = 0
for _name, _w in (("aq", 512), ("ak", 512), ("av", 512), ("bq", 256), ("bkv", 288), ("cq", 512), ("ck", 128),
                  ("cv", 128), ("dq", 512), ("dk", 128), ("dv", 128), ("g", 4096)):
    _OFF[_name] = (_o, _o + _w)
    _o += _w
_PAIR_HEAD_ORDER = (0, 4, 1, 5, 2, 6, 3, 7)


def _dot(a, b):
    return jnp.dot(a, b, preferred_element_type=f32)


def _dot_nt(a, b):
    return lax.dot_general(a, b, (((1,), (1,)), ((), ())), preferred_element_type=f32)


def _sigmoid(v):
    return 1.0 / (1.0 + jnp.exp(-v))


def _rms(v):
    return v * lax.rsqrt(jnp.mean(v * v, axis=-1, keepdims=True) + EPS)


def _modulate(v, shift, scale):
    return _rms(v) * (1.0 + scale) + shift


def _const_spec(shape):
    zeros = (0,) * len(shape)
    return pl.BlockSpec(shape, lambda *_: zeros, pipeline_mode=pl.Buffered(1))


def _mod_kernel(c_ref, w_ref, b_ref, o_ref):
    cv = c_ref[...]
    a = cv * _sigmoid(cv)
    o_ref[...] = _dot(a, w_ref[...]) + b_ref[...]


def _mod_call(cs, w_mod, b_mod):
    depth, d, width = w_mod.shape
    tn = 1536
    return pl.pallas_call(
        _mod_kernel,
        grid=(depth, width // tn),
        in_specs=[
            pl.BlockSpec((8, d), lambda l, j: (0, 0)),
            pl.BlockSpec((None, d, tn), lambda l, j: (l, 0, j)),
            pl.BlockSpec((None, 1, tn), lambda l, j: (l, 0, j)),
        ],
        out_specs=pl.BlockSpec((None, 8, tn), lambda l, j: (l, 0, j)),
        out_shape=jax.ShapeDtypeStruct((depth, 8, width), f32),
        compiler_params=pltpu.CompilerParams(vmem_limit_bytes=VMEM_LIMIT),
    )(cs, w_mod, b_mod.reshape(depth, 1, width))


def _proj_kernel(x_ref, mod_ref, c64_ref, s64_ref, cb_ref, sb_ref, w1_ref, qn_ref, kvn_ref, wuq_ref, wk_ref,
                 wv_ref, gq_ref, gk_ref,
                 qa_ref, ka_ref, va_ref, qb_ref, kb_ref, vb_ref, qc_ref, kc_ref, vc_ref, qd_ref, kd_ref, vd_ref,
                 *, scale_b):
    tm = x_ref.shape[0]
    h = _modulate(x_ref[...], mod_ref[0:1, :], mod_ref[1:2, :]).astype(bf16)
    lane = lax.broadcasted_iota(jnp.int32, (tm, LANES), 1)
    even = (lane & 1) == 0
    lo = lane < HALF
    c64, s64, cb, sb = c64_ref[...], s64_ref[...], cb_ref[...], sb_ref[...]

    def rope(v, cos, sin):
        partner = jnp.where(even, pltpu.roll(v, LANES - 1, 1), pltpu.roll(v, 1, 1))
        return v * cos + partner * sin

    def head_norm(v, gain):
        sq = v * v
        s_lo = jnp.sum(jnp.where(lo, sq, 0.0), axis=-1, keepdims=True)
        s_hi = jnp.sum(jnp.where(lo, 0.0, sq), axis=-1, keepdims=True)
        inv = jnp.where(lo, lax.rsqrt(s_lo * (1.0 / HALF) + EPS), lax.rsqrt(s_hi * (1.0 / HALF) + EPS))
        return v * inv * gain

    def chunk(z, j):
        return z[:, j * LANES:(j + 1) * LANES]

    qk_scale = DIFF_QK_DIM ** -0.5

    z = _dot(h, w1_ref[:, 0:1536])
    for j in range(4):
        qa_ref[j] = (rope(chunk(z, j), c64, s64) * qk_scale).astype(bf16)
        ka_ref[j] = rope(chunk(z, 4 + j), c64, s64).astype(bf16)
        va_ref[j] = chunk(z, 8 + j).astype(bf16)

    z = _dot(h, w1_ref[:, 1536:2176])
    dq = (_rms(z[:, 0:256]) * qn_ref[...]).astype(bf16)
    ckv = (_rms(z[:, 256:512]) * kvn_ref[...]).astype(bf16)
    k_rot = rope(z[:, 512:640], cb, sb)
    zq = _dot(dq, wuq_ref[...])
    zk = _dot(ckv, wk_ref[...])
    zv = _dot(ckv, wv_ref[...])
    for j in range(MLA_HEADS):
        qb_ref[j] = (rope(chunk(zq, j), cb, sb) * scale_b).astype(bf16)
        kb_ref[j] = (chunk(zk, j) + k_rot).astype(bf16)
    for j in range(4):
        vb_ref[j] = chunk(zv, j).astype(bf16)

    z = _dot(h, w1_ref[:, 2176:2944])
    gq, gk = gq_ref[...], gk_ref[...]
    for j in range(4):
        qc_ref[j] = (rope(head_norm(chunk(z, j), gq), c64, s64) * qk_scale).astype(bf16)
    kc_ref[0] = rope(head_norm(chunk(z, 4), gk), c64, s64).astype(bf16)
    vc_ref[0] = chunk(z, 5).astype(bf16)

    z = _dot(h, w1_ref[:, 2944:3712])
    for j in range(4):
        qd_ref[j] = (rope(chunk(z, j), c64, s64) * qk_scale).astype(bf16)
    kd_ref[0] = rope(chunk(z, 4), c64, s64).astype(bf16)
    vd_ref[0] = chunk(z, 5).astype(bf16)


def _proj_call(xs, mod, tables, w, tm):
    b, t, d = xs.shape
    c64, s64, cb, sb = tables
    grid = (b, t // tm)

    def out(nchunks):
        return (jax.ShapeDtypeStruct((b, nchunks, t, LANES), bf16),
                pl.BlockSpec((None, nchunks, tm, LANES), lambda bi, i: (bi, 0, i, 0)))

    outs = [out(n) for n in (4, 4, 4, 8, 8, 4, 4, 1, 1, 4, 1, 1)]
    tab_spec = pl.BlockSpec((tm, LANES), lambda bi, i: (i, 0))
    scale_b = (MLA_NOPE_DIM + MLA_ROPE_DIM) ** -0.5
    return pl.pallas_call(
        functools.partial(_proj_kernel, scale_b=scale_b),
        grid=grid,
        in_specs=[
            pl.BlockSpec((None, tm, d), lambda bi, i: (bi, i, 0)),
            pl.BlockSpec((None, N_MOD, d), lambda bi, i: (bi, 0, 0)),
            tab_spec, tab_spec, tab_spec, tab_spec,
            _const_spec(w["w1"].shape), _const_spec((1, 256)), _const_spec((1, 256)),
            _const_spec(w["wuq"].shape), _const_spec(w["wk"].shape), _const_spec(w["wv"].shape),
            _const_spec((1, LANES)), _const_spec((1, LANES)),
        ],
        out_specs=[o[1] for o in outs],
        out_shape=[o[0] for o in outs],
        compiler_params=pltpu.CompilerParams(vmem_limit_bytes=VMEM_LIMIT),
    )(xs, mod, c64, s64, cb, sb, w["w1"], w["qn"], w["kvn"], w["wuq"], w["wk"], w["wv"], w["gq"], w["gk"])


def _flash_kernel(*refs, mode, has_x, has_sink, tk, lam_init):
    refs = list(refs)
    q_ref, kc_ref, vc_ref = refs[:3]
    pos = 3
    if has_x:
        k_ref, v_ref = refs[pos:pos + 2]
        pos += 2
    if mode == "A":
        dl_ref, subln_ref = refs[pos:pos + 2]
        pos += 2
    if has_sink:
        sink_ref = refs[pos]
        pos += 1
    o_ref = refs[pos]

    tq = q_ref.shape[1]
    lane = lax.broadcasted_iota(jnp.int32, (tq, LANES), 1)
    lo = lane < HALF
    if mode == "B":
        qs = (q_ref[0], q_ref[1])
        k_idx = (0, 1)
    else:
        q = q_ref[0]
        zero = jnp.zeros_like(q)
        qs = (jnp.where(lo, q, zero), jnp.where(lo, zero, q))
        k_idx = (0, 0)

    def update(state, qv, kblk, vblk):
        m, l, acc = state
        s = _dot_nt(qv, kblk)
        m_new = jnp.maximum(m, jnp.max(s, axis=-1, keepdims=True))
        alpha = jnp.exp(m - m_new)
        p = jnp.exp(s - m_new)
        l = alpha * l + jnp.sum(p, axis=-1, keepdims=True)
        acc = alpha * acc + _dot(p.astype(bf16), vblk)
        return m_new, l, acc

    states = []
    for u in range(2):
        if has_sink:
            sink = sink_ref[pl.program_id(1) + 4 * u]
            m0 = jnp.full((tq, 1), sink, f32)
            l0 = jnp.ones((tq, 1), f32)
        else:
            m0 = jnp.full((tq, 1), NEG_INF, f32)
            l0 = jnp.zeros((tq, 1), f32)
        st = (m0, l0, jnp.zeros((tq, LANES), f32))
        states.append(update(st, qs[u], kc_ref[k_idx[u]], vc_ref[0]))

    if has_x:
        n_chunks = k_ref.shape[1] // tk

        def body(c, carry):
            start = pl.multiple_of(c * tk, tk)
            vblk = v_ref[0, pl.ds(start, tk), :]
            new = []
            for u in range(2):
                kblk = k_ref[k_idx[u], pl.ds(start, tk), :]
                new.append(update(carry[u], qs[u], kblk, vblk))
            return tuple(new)

        states = lax.fori_loop(0, n_chunks, body, tuple(states))

    o0 = states[0][2] / states[0][1]
    o1 = states[1][2] / states[1][1]
    if mode == "A":
        dl = dl_ref[...]
        lam = (jnp.exp(jnp.sum(dl[0:1] * dl[1:2], axis=-1, keepdims=True))
               - jnp.exp(jnp.sum(dl[2:3] * dl[3:4], axis=-1, keepdims=True)) + lam_init)
        o = _rms(o0 - lam * o1) * subln_ref[...] * (1.0 - lam_init)
    else:
        o = jnp.where(lo, o0, o1)
    o_ref[...] = o.astype(bf16)


def _flash_call(mode, q, kc, vc, k, v, *, tq, tk, extras=(), sink=None, lam_init=0.0):
    b, _, t, _ = q.shape
    l_ctx = kc.shape[2]
    has_x = k is not None
    gq = 2 if mode == "B" else 1
    shared_kv = mode == "C"

    def kv_idx(bi, g, i):
        return (bi, 0 if shared_kv else g, 0, 0)

    in_specs = [
        pl.BlockSpec((None, gq, tq, LANES), lambda bi, g, i: (bi, g, i, 0)),
        pl.BlockSpec((None, gq, l_ctx, LANES), kv_idx),
        pl.BlockSpec((None, 1, l_ctx, LANES), kv_idx),
    ]
    args = [q, kc, vc]
    if has_x:
        n = k.shape[2]
        in_specs += [pl.BlockSpec((None, gq, n, LANES), kv_idx), pl.BlockSpec((None, 1, n, LANES), kv_idx)]
        args += [k, v]
    for e in extras:
        in_specs.append(pl.BlockSpec(e.shape, lambda bi, g, i, nd=e.ndim: (0,) * nd))
        args.append(e)
    if sink is not None:
        in_specs.append(pl.BlockSpec(memory_space=pltpu.SMEM))
        args.append(sink)
    return pl.pallas_call(
        functools.partial(_flash_kernel, mode=mode, has_x=has_x, has_sink=sink is not None, tk=tk,
                          lam_init=lam_init),
        grid=(b, 4, t // tq),
        in_specs=in_specs,
        out_specs=pl.BlockSpec((None, tq, LANES), lambda bi, g, i: (bi, i, g)),
        out_shape=jax.ShapeDtypeStruct((b, t, 4 * LANES), bf16),
        compiler_params=pltpu.CompilerParams(vmem_limit_bytes=VMEM_LIMIT),
    )(*args)


def _window_kernel(q_ref, kc_ref, vc_ref, k_ref, v_ref, sink_ref, o_ref):
    tq = q_ref.shape[1]
    n = k_ref.shape[1]
    band = tq + 2 * WINDOW
    t0 = pl.program_id(1) * tq
    start = pl.multiple_of(jnp.clip(t0 - WINDOW, 0, n - band), WINDOW)
    kband = k_ref[0, pl.ds(start, band), :]
    vband = v_ref[0, pl.ds(start, band), :]
    kctx, vctx = kc_ref[0], vc_ref[0]
    kpos = start + lax.broadcasted_iota(jnp.int32, (tq, band), 1)
    qpos = t0 + lax.broadcasted_iota(jnp.int32, (tq, band), 0)
    valid = jnp.abs(kpos - qpos) <= WINDOW
    lane = lax.broadcasted_iota(jnp.int32, (tq, LANES), 1)
    lo = lane < HALF
    for j in range(4):
        q = q_ref[j]
        zero = jnp.zeros_like(q)
        outs = []
        for u in range(2):
            qv = jnp.where(lo, q, zero) if u == 0 else jnp.where(lo, zero, q)
            sink = sink_ref[j + 4 * u]
            s_ctx = _dot_nt(qv, kctx)
            s_band = jnp.where(valid, _dot_nt(qv, kband), NEG_INF)
            m = jnp.maximum(jnp.maximum(jnp.max(s_ctx, axis=-1, keepdims=True),
                                        jnp.max(s_band, axis=-1, keepdims=True)), sink)
            p_ctx = jnp.exp(s_ctx - m)
            p_band = jnp.exp(s_band - m)
            l = (jnp.sum(p_ctx, axis=-1, keepdims=True) + jnp.sum(p_band, axis=-1, keepdims=True)
                 + jnp.exp(sink - m))
            acc = _dot(p_ctx.astype(bf16), vctx) + _dot(p_band.astype(bf16), vband)
            outs.append(acc / l)
        o_ref[:, j * LANES:(j + 1) * LANES] = jnp.where(lo, outs[0], outs[1]).astype(bf16)


def _window_call(q, kc, vc, k, v, sink, *, tq):
    b, _, n, _ = q.shape
    l_ctx = kc.shape[2]
    return pl.pallas_call(
        _window_kernel,
        grid=(b, n // tq),
        in_specs=[
            pl.BlockSpec((None, 4, tq, LANES), lambda bi, i: (bi, 0, i, 0)),
            pl.BlockSpec((None, 1, l_ctx, LANES), lambda bi, i: (bi, 0, 0, 0)),
            pl.BlockSpec((None, 1, l_ctx, LANES), lambda bi, i: (bi, 0, 0, 0)),
            pl.BlockSpec((None, 1, n, LANES), lambda bi, i: (bi, 0, 0, 0)),
            pl.BlockSpec((None, 1, n, LANES), lambda bi, i: (bi, 0, 0, 0)),
            pl.BlockSpec(memory_space=pltpu.SMEM),
        ],
        out_specs=pl.BlockSpec((None, tq, 4 * LANES), lambda bi, i: (bi, i, 0)),
        out_shape=jax.ShapeDtypeStruct((b, n, 4 * LANES), bf16),
        compiler_params=pltpu.CompilerParams(vmem_limit_bytes=VMEM_LIMIT),
    )(q, kc, vc, k, v, sink)


def _merge_kernel(x_ref, mod_ref, oa_ref, ob_ref, oc_ref, od_ref, wg_ref, wbr_ref, wout_ref, o_ref):
    x = x_ref[...]
    h = _modulate(x, mod_ref[0:1, :], mod_ref[1:2, :]).astype(bf16)
    mix = None
    for k, br_ref in enumerate((oa_ref, ob_ref, oc_ref, od_ref)):
        g = _dot(h, wg_ref[:, k * D_MODEL:(k + 1) * D_MODEL])
        y = _dot(br_ref[...], wbr_ref[k])
        term = _sigmoid(g) * y
        mix = term if mix is None else mix + term
    o_ref[...] = x + mod_ref[2:3, :] * _dot(mix.astype(bf16), wout_ref[...])


def _merge_call(xs, mod, branches, w, tm):
    b, t, d = xs.shape
    tiles = t // tm
    x2 = xs.reshape(b * t, d)
    br = [o.reshape(b * t, BRANCH_WIDTH) for o in branches]
    br_spec = pl.BlockSpec((tm, BRANCH_WIDTH), lambda i: (i, 0))
    out = pl.pallas_call(
        _merge_kernel,
        grid=(b * tiles,),
        in_specs=[
            pl.BlockSpec((tm, d), lambda i: (i, 0)),
            pl.BlockSpec((None, N_MOD, d), lambda i: (i // tiles, 0, 0)),
            br_spec, br_spec, br_spec, br_spec,
            _const_spec(w["wg"].shape), _const_spec(w["wbr"].shape), _const_spec(w["wout"].shape),
        ],
        out_specs=pl.BlockSpec((tm, d), lambda i: (i, 0)),
        out_shape=jax.ShapeDtypeStruct((b * t, d), f32),
        compiler_params=pltpu.CompilerParams(vmem_limit_bytes=VMEM_LIMIT),
    )(x2, mod, *br, w["wg"], w["wbr"], w["wout"])
    return out.reshape(b, t, d)


def _ffn_kernel(x_ref, xp_ref, xn_ref, mod_ref, wup_ref, cw_ref, cb_ref, wdn_ref, fin_ref, o_ref, *, seq_len,
                final_norm):
    tm = x_ref.shape[0]
    rows = tm + 2 * HALO
    x = x_ref[...]
    xe = jnp.concatenate([xp_ref[...], x, xn_ref[...]], axis=0)
    h = _modulate(xe, mod_ref[3:4, :], mod_ref[4:5, :]).astype(bf16)
    pos = (pl.program_id(0) % (seq_len // tm)) * tm + lax.broadcasted_iota(jnp.int32, (tm, 1), 0)
    first = pos == 0
    last = pos == seq_len - 1

    def conv(u, c0):
        prev = jnp.where(first, 0.0, pltpu.roll(u, 1, 0)[HALO:HALO + tm])
        nxt = jnp.where(last, 0.0, pltpu.roll(u, rows - 1, 0)[HALO:HALO + tm])
        cur = u[HALO:HALO + tm]
        cols = slice(c0, c0 + FFN_CHUNK)
        return prev * cw_ref[0:1, cols] + cur * cw_ref[1:2, cols] + nxt * cw_ref[2:3, cols] + cb_ref[:, cols]

    acc = None
    for j in range(D_FF // FFN_CHUNK):
        cv, cg = j * FFN_CHUNK, D_FF + j * FFN_CHUNK
        val = conv(_dot(h, wup_ref[:, cv:cv + FFN_CHUNK]), cv)
        gate = conv(_dot(h, wup_ref[:, cg:cg + FFN_CHUNK]), cg)
        a = (gate * _sigmoid(gate) * val).astype(bf16)
        part = _dot(a, wdn_ref[cv:cv + FFN_CHUNK, :])
        acc = part if acc is None else acc + part
    out = x + mod_ref[5:6, :] * acc
    if final_norm:
        out = _rms(out) * fin_ref[...]
    o_ref[...] = out


def _ffn_call(xs, mod, w, fin, tm, final_norm):
    b, t, d = xs.shape
    tiles = t // tm
    x2 = xs.reshape(b * t, d)
    hb = tm // HALO
    last_halo = b * t // HALO - 1
    out = pl.pallas_call(
        functools.partial(_ffn_kernel, seq_len=t, final_norm=final_norm),
        grid=(b * tiles,),
        in_specs=[
            pl.BlockSpec((tm, d), lambda i: (i, 0)),
            pl.BlockSpec((HALO, d), lambda i: (jnp.maximum(i * hb - 1, 0), 0)),
            pl.BlockSpec((HALO, d), lambda i: (jnp.minimum((i + 1) * hb, last_halo), 0)),
            pl.BlockSpec((None, N_MOD, d), lambda i: (i // tiles, 0, 0)),
            _const_spec(w["wup"].shape), _const_spec(w["cw"].shape), _const_spec(w["cb"].shape),
            _const_spec(w["wdn"].shape), _const_spec((1, d)),
        ],
        out_specs=pl.BlockSpec((tm, d), lambda i: (i, 0)),
        out_shape=jax.ShapeDtypeStruct((b * t, d), f32),
        compiler_params=pltpu.CompilerParams(vmem_limit_bytes=VMEM_LIMIT),
    )(x2, x2, x2, mod, w["wup"], w["cw"], w["cb"], w["wdn"], fin)
    return out.reshape(b, t, d)


def _cols(w, name):
    a, b = _OFF[name]
    return w[:, a:b]


def _pair_heads(w_cols):
    d = w_cols.shape[0]
    return w_cols.reshape(d, GQA_HEADS, HALF)[:, jnp.array(_PAIR_HEAD_ORDER), :].reshape(d, GQA_HEADS * HALF)


def _layer_weights(l, w_in, mla_q_norm, mla_kv_norm, mla_w_uq, mla_w_ukv, gqa_q_norm, gqa_k_norm, w_branch,
                   w_out, ffn_w_up, ffn_conv_w, ffn_conv_b, ffn_w_down):
    wi = w_in[l]
    d = wi.shape[0]
    bkv = _cols(wi, "bkv")
    k_rot = jnp.concatenate([jnp.zeros((d, MLA_NOPE_DIM), f32), bkv[:, MLA_KV_LORA:],
                             jnp.zeros((d, LANES - MLA_NOPE_DIM - MLA_ROPE_DIM), f32)], axis=1)
    w1 = jnp.concatenate([
        _cols(wi, "aq"), _cols(wi, "ak"), _cols(wi, "av"),
        _cols(wi, "bq"), bkv[:, :MLA_KV_LORA], k_rot,
        _pair_heads(_cols(wi, "cq")), _cols(wi, "ck"), _cols(wi, "cv"),
        _pair_heads(_cols(wi, "dq")), _cols(wi, "dk"), _cols(wi, "dv")], axis=1).astype(bf16)
    hq = MLA_NOPE_DIM + MLA_ROPE_DIM
    wuq = mla_w_uq[l].reshape(MLA_Q_LORA, MLA_HEADS, hq)
    wuq = jnp.pad(wuq, ((0, 0), (0, 0), (0, LANES - hq))).reshape(MLA_Q_LORA, MLA_HEADS * LANES).astype(bf16)
    wukv = mla_w_ukv[l].reshape(MLA_KV_LORA, MLA_HEADS, MLA_NOPE_DIM + MLA_V_DIM)
    wk = jnp.pad(wukv[:, :, :MLA_NOPE_DIM], ((0, 0), (0, 0), (0, LANES - MLA_NOPE_DIM)))
    wk = wk.reshape(MLA_KV_LORA, MLA_HEADS * LANES).astype(bf16)
    wv = wukv[:, :, MLA_NOPE_DIM:].reshape(MLA_KV_LORA, MLA_HEADS * MLA_V_DIM).astype(bf16)
    wb = w_branch[l]
    order = jnp.array(_PAIR_HEAD_ORDER)
    paired_rows = lambda m: m.reshape(GQA_HEADS, HALF, -1)[order].reshape(BRANCH_WIDTH, -1)
    wbr = jnp.stack([wb[0], wb[1], paired_rows(wb[2]), paired_rows(wb[3])]).astype(bf16)
    return dict(
        w1=w1, wuq=wuq, wk=wk, wv=wv,
        qn=mla_q_norm[l].reshape(1, -1), kvn=mla_kv_norm[l].reshape(1, -1),
        gq=jnp.tile(gqa_q_norm[l], 2).reshape(1, LANES), gk=jnp.tile(gqa_k_norm[l], 2).reshape(1, LANES),
        wg=_cols(wi, "g").astype(bf16), wbr=wbr, wout=w_out[l].astype(bf16),
        wup=ffn_w_up[l].astype(bf16), cw=ffn_conv_w[l], cb=ffn_conv_b[l].reshape(1, -1),
        wdn=ffn_w_down[l].astype(bf16))


def _rope_tables(n, l_ctx):
    t = jnp.arange(n, dtype=jnp.int32)
    row, col = (t // GRID_W).astype(f32), (t % GRID_W).astype(f32)

    def expanded(rot_dim):
        axis_dim = rot_dim // 2
        inv = ROPE_THETA ** (-jnp.arange(0, axis_dim, 2, dtype=f32) / axis_dim)
        ang = jnp.concatenate([row[:, None] * inv, col[:, None] * inv], axis=-1)
        cos = jnp.repeat(jnp.cos(ang), 2, axis=-1)
        sin = jnp.repeat(jnp.sin(ang), 2, axis=-1) * jnp.tile(jnp.array([-1.0, 1.0], f32), rot_dim // 2)
        return cos, sin

    c64, s64 = expanded(2 * HALF // 2)
    c64, s64 = jnp.tile(c64, (1, 2)), jnp.tile(s64, (1, 2))
    c32, s32 = expanded(MLA_ROPE_DIM)
    pad = LANES - MLA_NOPE_DIM - MLA_ROPE_DIM
    cb = jnp.concatenate([jnp.ones((n, MLA_NOPE_DIM), f32), c32, jnp.ones((n, pad), f32)], axis=1)
    sb = jnp.concatenate([jnp.zeros((n, MLA_NOPE_DIM), f32), s32, jnp.zeros((n, pad), f32)], axis=1)
    ones, zeros = jnp.ones((l_ctx, LANES), f32), jnp.zeros((l_ctx, LANES), f32)
    return (c64, s64, cb, sb), (ones, zeros, ones, zeros)


def kernel(x, c, ctx, c_ctx, w_mod, b_mod, w_in, diff_lambda, diff_subln, mla_q_norm, mla_kv_norm, mla_w_uq,
           mla_w_ukv, gqa_q_norm, gqa_k_norm, swa_sink, w_branch, w_out, ffn_w_up, ffn_conv_w, ffn_conv_b,
           ffn_w_down, final_norm):
    b, n, d = x.shape
    l_ctx = ctx.shape[1]
    depth = w_mod.shape[0]
    tm_x = min(512, n)
    tq = min(256, n)
    tk = min(512, n)

    cs = jnp.zeros((8, d), f32).at[:b].set(c).at[b].set(c_ctx)
    mods = _mod_call(cs, w_mod, b_mod).reshape(depth, 8, N_MOD, d)
    tab_x, tab_c = _rope_tables(n, l_ctx)
    fin = final_norm.reshape(1, d)

    xs, xc = x, ctx
    for l in range(depth):
        w = _layer_weights(l, w_in, mla_q_norm, mla_kv_norm, mla_w_uq, mla_w_ukv, gqa_q_norm, gqa_k_norm,
                           w_branch, w_out, ffn_w_up, ffn_conv_w, ffn_conv_b, ffn_w_down)
        mod_x = mods[l, :b]
        mod_c = jnp.broadcast_to(mods[l, b][None], (b, N_MOD, d))
        lam_init = 0.8 - 0.6 * math.exp(-0.3 * l)
        a_extras = (diff_lambda[l], diff_subln[l].reshape(1, LANES))
        sink = swa_sink[l]
        update_ctx = l < depth - 1

        px = _proj_call(xs, mod_x, tab_x, w, tm_x)
        pc = _proj_call(xc, mod_c, tab_c, w, l_ctx)
        qa, ka, va, qb, kb, vb, qc, kc, vc, qd, kd, vd = px
        qa_c, ka_c, va_c, qb_c, kb_c, vb_c, qc_c, kc_c, vc_c, qd_c, kd_c, vd_c = pc

        oa = _flash_call("A", qa, ka_c, va_c, ka, va, tq=tq, tk=tk, extras=a_extras, lam_init=lam_init)
        ob = _flash_call("B", qb, kb_c, vb_c, kb, vb, tq=tq, tk=tk)
        oc = _flash_call("C", qc, kc_c, vc_c, kc, vc, tq=tq, tk=tk)
        od = _window_call(qd, kd_c, vd_c, kd, vd, sink, tq=tq)
        xs_mid = _merge_call(xs, mod_x, (oa, ob, oc, od), w, tm_x)
        xs_new = _ffn_call(xs_mid, mod_x, w, fin, tm_x, final_norm=not update_ctx)

        if update_ctx:
            oa = _flash_call("A", qa_c, ka_c, va_c, None, None, tq=l_ctx, tk=tk, extras=a_extras,
                             lam_init=lam_init)
            ob = _flash_call("B", qb_c, kb_c, vb_c, None, None, tq=l_ctx, tk=tk)
            oc = _flash_call("C", qc_c, kc_c, vc_c, None, None, tq=l_ctx, tk=tk)
            od = _flash_call("C", qd_c, kd_c, vd_c, None, None, tq=l_ctx, tk=tk, sink=sink)
            xc_mid = _merge_call(xc, mod_c, (oa, ob, oc, od), w, l_ctx)
            xc = _ffn_call(xc_mid, mod_c, w, fin, l_ctx, final_norm=False)
        xs = xs_new
    return xs
```

```python
import functools
import math

import jax
import jax.numpy as jnp
from jax import lax
from jax.experimental import pallas as pl
from jax.experimental.pallas import tpu as pltpu

f32 = jnp.float32
bf16 = jnp.bfloat16

LANES = 128
HALF = 64
D_MODEL = 1024
GRID_W = 64
WINDOW = 128
N_BRANCH = 4
N_MOD = 6
EPS = 1e-6
ROPE_THETA = 10000.0
NEG_INF = -1e30
LOG2E = math.log2(math.e)
DIFF_HEADS = 4
DIFF_QK_DIM = 64
MLA_HEADS = 8
MLA_Q_LORA = 256
MLA_KV_LORA = 256
MLA_NOPE_DIM = 64
MLA_ROPE_DIM = 32
MLA_V_DIM = 64
GQA_HEADS = 8
BRANCH_WIDTH = 512
D_FF = 2816
FFN_CHUNK = 256
HALO = 8
VMEM_LIMIT = 56 * 1024 * 1024

_OFF = {}
_o = 0
for _name, _w in (("aq", 512), ("ak", 512), ("av", 512), ("bq", 256), ("bkv", 288), ("cq", 512), ("ck", 128),
                  ("cv", 128), ("dq", 512), ("dk", 128), ("dv", 128), ("g", 4096)):
    _OFF[_name] = (_o, _o + _w)
    _o += _w
_PAIR_HEAD_ORDER = (0, 4, 1, 5, 2, 6, 3, 7)


def _dot(a, b):
    return jnp.dot(a, b, preferred_element_type=f32)


def _dot_nt(a, b):
    return lax.dot_general(a, b, (((1,), (1,)), ((), ())), preferred_element_type=f32)


def _sigmoid(v):
    return 1.0 / (1.0 + jnp.exp(-v))


def _rms(v):
    return v * lax.rsqrt(jnp.mean(v * v, axis=-1, keepdims=True) + EPS)


def _modulate(v, shift, scale):
    return _rms(v) * (1.0 + scale) + shift


def _const_spec(shape):
    zeros = (0,) * len(shape)
    return pl.BlockSpec(shape, lambda *_: zeros, pipeline_mode=pl.Buffered(1))


def _mod_kernel(c_ref, w_ref, b_ref, o_ref):
    cv = c_ref[...]
    a = cv * _sigmoid(cv)
    o_ref[...] = _dot(a, w_ref[...]) + b_ref[...]


def _mod_call(cs, w_mod, b_mod):
    depth, d, width = w_mod.shape
    tn = 1536
    return pl.pallas_call(
        _mod_kernel,
        name="mod_vectors",
        grid=(depth, width // tn),
        in_specs=[
            pl.BlockSpec((8, d), lambda l, j: (0, 0)),
            pl.BlockSpec((None, d, tn), lambda l, j: (l, 0, j)),
            pl.BlockSpec((None, 1, tn), lambda l, j: (l, 0, j)),
        ],
        out_specs=pl.BlockSpec((None, 8, tn), lambda l, j: (l, 0, j)),
        out_shape=jax.ShapeDtypeStruct((depth, 8, width), f32),
        compiler_params=pltpu.CompilerParams(vmem_limit_bytes=VMEM_LIMIT),
    )(cs, w_mod, b_mod.reshape(depth, 1, width))


def _proj_kernel(x_ref, mod_ref, c64_ref, s64_ref, cb_ref, sb_ref, w1_ref, qn_ref, kvn_ref, wuq_ref, wk_ref,
                 wv_ref, gq_ref, gk_ref,
                 qa_ref, ka_ref, va_ref, qb_ref, kb_ref, vb_ref, qc_ref, kc_ref, vc_ref, qd_ref, kd_ref, vd_ref,
                 *, scale_b):
    tm = x_ref.shape[0]
    h = _modulate(x_ref[...], mod_ref[0:1, :], mod_ref[1:2, :]).astype(bf16)
    lane = lax.broadcasted_iota(jnp.int32, (tm, LANES), 1)
    even = (lane & 1) == 0
    lo = lane < HALF
    c64, s64, cb, sb = c64_ref[...], s64_ref[...], cb_ref[...], sb_ref[...]

    def rope(v, cos, sin):
        partner = jnp.where(even, pltpu.roll(v, LANES - 1, 1), pltpu.roll(v, 1, 1))
        return v * cos + partner * sin

    def head_norm(v, gain):
        sq = v * v
        s_lo = jnp.sum(jnp.where(lo, sq, 0.0), axis=-1, keepdims=True)
        s_hi = jnp.sum(jnp.where(lo, 0.0, sq), axis=-1, keepdims=True)
        inv = jnp.where(lo, lax.rsqrt(s_lo * (1.0 / HALF) + EPS), lax.rsqrt(s_hi * (1.0 / HALF) + EPS))
        return v * inv * gain

    def chunk(z, j):
        return z[:, j * LANES:(j + 1) * LANES]

    qk_scale = DIFF_QK_DIM ** -0.5 * LOG2E

    z = _dot(h, w1_ref[:, 0:1536])
    for j in range(4):
        qa_ref[j] = (rope(chunk(z, j), c64, s64) * qk_scale).astype(bf16)
        ka_ref[j] = rope(chunk(z, 4 + j), c64, s64).astype(bf16)
        va_ref[j] = chunk(z, 8 + j).astype(bf16)

    z = _dot(h, w1_ref[:, 1536:2176])
    dq = (_rms(z[:, 0:256]) * qn_ref[...]).astype(bf16)
    ckv = (_rms(z[:, 256:512]) * kvn_ref[...]).astype(bf16)
    k_rot = rope(z[:, 512:640], cb, sb)
    zq = _dot(dq, wuq_ref[...])
    zk = _dot(ckv, wk_ref[...])
    zv = _dot(ckv, wv_ref[...])
    for j in range(MLA_HEADS):
        qb_ref[j] = (rope(chunk(zq, j), cb, sb) * scale_b).astype(bf16)
        kb_ref[j] = (chunk(zk, j) + k_rot).astype(bf16)
    for j in range(4):
        vb_ref[j] = chunk(zv, j).astype(bf16)

    z = _dot(h, w1_ref[:, 2176:2944])
    gq, gk = gq_ref[...], gk_ref[...]
    for j in range(4):
        qc_ref[j] = (rope(head_norm(chunk(z, j), gq), c64, s64) * qk_scale).astype(bf16)
    kc_ref[0] = rope(head_norm(chunk(z, 4), gk), c64, s64).astype(bf16)
    vc_ref[0] = chunk(z, 5).astype(bf16)

    z = _dot(h, w1_ref[:, 2944:3712])
    for j in range(4):
        qd_ref[j] = (rope(chunk(z, j), c64, s64) * qk_scale).astype(bf16)
    kd_ref[0] = rope(chunk(z, 4), c64, s64).astype(bf16)
    vd_ref[0] = chunk(z, 5).astype(bf16)


def _proj_call(xs, mod, tables, w, tm):
    b, t, d = xs.shape
    c64, s64, cb, sb = tables
    grid = (b, t // tm)

    def out(nchunks):
        return (jax.ShapeDtypeStruct((b, nchunks, t, LANES), bf16),
                pl.BlockSpec((None, nchunks, tm, LANES), lambda bi, i: (bi, 0, i, 0)))

    outs = [out(n) for n in (4, 4, 4, 8, 8, 4, 4, 1, 1, 4, 1, 1)]
    tab_spec = pl.BlockSpec((tm, LANES), lambda bi, i: (i, 0))
    scale_b = (MLA_NOPE_DIM + MLA_ROPE_DIM) ** -0.5 * LOG2E
    return pl.pallas_call(
        functools.partial(_proj_kernel, scale_b=scale_b),
        name="in_proj",
        grid=grid,
        in_specs=[
            pl.BlockSpec((None, tm, d), lambda bi, i: (bi, i, 0)),
            pl.BlockSpec((None, N_MOD, d), lambda bi, i: (bi, 0, 0)),
            tab_spec, tab_spec, tab_spec, tab_spec,
            _const_spec(w["w1"].shape), _const_spec((1, 256)), _const_spec((1, 256)),
            _const_spec(w["wuq"].shape), _const_spec(w["wk"].shape), _const_spec(w["wv"].shape),
            _const_spec((1, LANES)), _const_spec((1, LANES)),
        ],
        out_specs=[o[1] for o in outs],
        out_shape=[o[0] for o in outs],
        compiler_params=pltpu.CompilerParams(vmem_limit_bytes=VMEM_LIMIT),
    )(xs, mod, c64, s64, cb, sb, w["w1"], w["qn"], w["kvn"], w["wuq"], w["wk"], w["wv"], w["gq"], w["gk"])


def _flash_kernel(*refs, mode, has_x, has_sink, tk, lam_init):
    refs = list(refs)
    q_ref, kc_ref, vc_ref = refs[:3]
    pos = 3
    if has_x:
        k_ref, v_ref = refs[pos:pos + 2]
        pos += 2
    if mode == "A":
        dl_ref, subln_ref = refs[pos:pos + 2]
        pos += 2
    if has_sink:
        sink_ref = refs[pos]
        pos += 1
    o_ref, q2_ref, m_ref, l_ref, acc_ref, sc_ref, pc_ref, alc_ref = refs[pos:pos + 8]
    if has_x:
        sa_ref, sb_ref, pa_ref, pb_ref, ala_ref, alb_ref = refs[pos + 8:pos + 14]

    tq = q_ref.shape[1]
    lane = lax.broadcasted_iota(jnp.int32, (tq, LANES), 1)
    lo = lane < HALF
    shared_k = mode != "B"
    if shared_k:
        q = q_ref[0]
        zero = jnp.zeros_like(q)
        q2_ref[0:tq, :] = jnp.where(lo, q, zero)
        q2_ref[tq:2 * tq, :] = jnp.where(lo, zero, q)
    else:
        q2_ref[0:tq, :] = q_ref[0]
        q2_ref[tq:2 * tq, :] = q_ref[1]

    for u in range(2):
        if has_sink:
            sink = sink_ref[pl.program_id(1) + 4 * u] * LOG2E
            m_ref[u] = jnp.full((tq, LANES), sink, f32)
            l_ref[u] = jnp.where(lane == 0, 1.0, 0.0)
        else:
            m_ref[u] = jnp.full((tq, LANES), NEG_INF, f32)
            l_ref[u] = jnp.zeros((tq, LANES), f32)
        acc_ref[u] = jnp.zeros((tq, LANES), f32)

    def scores(kref, start, size, s_ref):
        if shared_k:
            s_ref[...] = _dot_nt(q2_ref[...], kref[0, pl.ds(start, size), :])
        else:
            for u in range(2):
                rows = slice(u * tq, (u + 1) * tq)
                s_ref[rows, :] = _dot_nt(q2_ref[rows, :], kref[u, pl.ds(start, size), :])

    def softmax(s_ref, p_ref, al_ref):
        for u in range(2):
            rows = slice(u * tq, (u + 1) * tq)
            cols = [s_ref[rows, j * LANES:(j + 1) * LANES] for j in range(s_ref.shape[1] // LANES)]
            m_prev = m_ref[u]
            m_new = jnp.maximum(m_prev, jnp.max(functools.reduce(jnp.maximum, cols), axis=-1, keepdims=True))
            alpha = jnp.exp2(m_prev - m_new)
            ps = [jnp.exp2(cj - m_new) for cj in cols]
            l_ref[u] = alpha * l_ref[u] + functools.reduce(lambda a, b: a + b, ps)
            for j, pj in enumerate(ps):
                p_ref[rows, j * LANES:(j + 1) * LANES] = pj.astype(bf16)
            al_ref[u] = alpha
            m_ref[u] = m_new

    def pv(p_ref, al_ref, vblk):
        for u in range(2):
            acc_ref[u] = al_ref[u] * acc_ref[u] + _dot(p_ref[u * tq:(u + 1) * tq, :], vblk)

    def keys(c):
        return pl.ds(pl.multiple_of(c * tk, tk), tk)

    scores(kc_ref, 0, kc_ref.shape[1], sc_ref)
    if has_x:
        n_chunks = k_ref.shape[1] // tk
        scores(k_ref, 0, tk, sa_ref)
    softmax(sc_ref, pc_ref, alc_ref)
    if not has_x:
        pv(pc_ref, alc_ref, vc_ref[0])
    else:
        scores(k_ref, tk, tk, sb_ref)
        softmax(sa_ref, pa_ref, ala_ref)
        pv(pc_ref, alc_ref, vc_ref[0])

        def body(cc, carry):
            c = 1 + 2 * cc
            scores(k_ref, pl.multiple_of((c + 1) * tk, tk), tk, sa_ref)
            softmax(sb_ref, pb_ref, alb_ref)
            pv(pa_ref, ala_ref, v_ref[0, keys(c - 1), :])
            scores(k_ref, pl.multiple_of((c + 2) * tk, tk), tk, sb_ref)
            softmax(sa_ref, pa_ref, ala_ref)
            pv(pb_ref, alb_ref, v_ref[0, keys(c), :])
            return carry

        lax.fori_loop(0, (n_chunks - 2) // 2, body, 0, unroll=True)
        softmax(sb_ref, pb_ref, alb_ref)
        pv(pa_ref, ala_ref, v_ref[0, keys(n_chunks - 2), :])
        pv(pb_ref, alb_ref, v_ref[0, keys(n_chunks - 1), :])

    o0 = acc_ref[0] / jnp.sum(l_ref[0], axis=-1, keepdims=True)
    o1 = acc_ref[1] / jnp.sum(l_ref[1], axis=-1, keepdims=True)
    if mode == "A":
        dl = dl_ref[...]
        lam = (jnp.exp(jnp.sum(dl[0:1] * dl[1:2], axis=-1, keepdims=True))
               - jnp.exp(jnp.sum(dl[2:3] * dl[3:4], axis=-1, keepdims=True)) + lam_init)
        o = _rms(o0 - lam * o1) * subln_ref[...] * (1.0 - lam_init)
    else:
        o = jnp.where(lo, o0, o1)
    o_ref[...] = o.astype(bf16)


def _flash_call(mode, q, kc, vc, k, v, *, tq, tk, extras=(), sink=None, lam_init=0.0):
    b, _, t, _ = q.shape
    l_ctx = kc.shape[2]
    has_x = k is not None
    gq = 2 if mode == "B" else 1
    shared_kv = mode == "C"

    def kv_idx(bi, g, i):
        return (bi, 0 if shared_kv else g, 0, 0)

    in_specs = [
        pl.BlockSpec((None, gq, tq, LANES), lambda bi, g, i: (bi, g, i, 0)),
        pl.BlockSpec((None, gq, l_ctx, LANES), kv_idx),
        pl.BlockSpec((None, 1, l_ctx, LANES), kv_idx),
    ]
    args = [q, kc, vc]
    if has_x:
        n = k.shape[2]
        in_specs += [pl.BlockSpec((None, gq, n, LANES), kv_idx), pl.BlockSpec((None, 1, n, LANES), kv_idx)]
        args += [k, v]
    for e in extras:
        in_specs.append(pl.BlockSpec(e.shape, lambda bi, g, i, nd=e.ndim: (0,) * nd))
        args.append(e)
    if sink is not None:
        in_specs.append(pl.BlockSpec(memory_space=pltpu.SMEM))
        args.append(sink)
    state = pltpu.VMEM((2, tq, LANES), f32)
    scratch = [pltpu.VMEM((2 * tq, LANES), bf16), state, state, state,
               pltpu.VMEM((2 * tq, l_ctx), f32), pltpu.VMEM((2 * tq, l_ctx), bf16), state]
    if has_x:
        assert (k.shape[2] // tk) % 2 == 0, "key chunks are consumed two per loop trip"
        s_buf, p_buf = pltpu.VMEM((2 * tq, tk), f32), pltpu.VMEM((2 * tq, tk), bf16)
        scratch += [s_buf, s_buf, p_buf, p_buf, state, state]
    return pl.pallas_call(
        functools.partial(_flash_kernel, mode=mode, has_x=has_x, has_sink=sink is not None, tk=tk,
                          lam_init=lam_init),
        name="flash_" + mode + ("" if has_x else "_ctx"),
        grid=(b, 4, t // tq),
        in_specs=in_specs,
        out_specs=pl.BlockSpec((None, tq, LANES), lambda bi, g, i: (bi, i, g)),
        out_shape=jax.ShapeDtypeStruct((b, t, 4 * LANES), bf16),
        scratch_shapes=scratch,
        compiler_params=pltpu.CompilerParams(vmem_limit_bytes=VMEM_LIMIT),
    )(*args)


def _window_kernel(q_ref, kc_ref, vc_ref, k_ref, v_ref, sink_ref, o_ref):
    tq = q_ref.shape[1]
    n = k_ref.shape[1]
    band = tq + 2 * WINDOW
    t0 = pl.program_id(1) * tq
    start = pl.multiple_of(jnp.clip(t0 - WINDOW, 0, n - band), WINDOW)
    kband = k_ref[0, pl.ds(start, band), :]
    vband = v_ref[0, pl.ds(start, band), :]
    kctx, vctx = kc_ref[0], vc_ref[0]
    kpos = start + lax.broadcasted_iota(jnp.int32, (tq, band), 1)
    qpos = t0 + lax.broadcasted_iota(jnp.int32, (tq, band), 0)
    valid = jnp.abs(kpos - qpos) <= WINDOW
    lane = lax.broadcasted_iota(jnp.int32, (tq, LANES), 1)
    lo = lane < HALF
    for j in range(4):
        q = q_ref[j]
        zero = jnp.zeros_like(q)
        outs = []
        for u in range(2):
            qv = jnp.where(lo, q, zero) if u == 0 else jnp.where(lo, zero, q)
            sink = sink_ref[j + 4 * u] * LOG2E
            s_ctx = _dot_nt(qv, kctx)
            s_band = jnp.where(valid, _dot_nt(qv, kband), NEG_INF)
            m = jnp.maximum(jnp.maximum(jnp.max(s_ctx, axis=-1, keepdims=True),
                                        jnp.max(s_band, axis=-1, keepdims=True)), sink)
            p_ctx = jnp.exp2(s_ctx - m)
            p_band = jnp.exp2(s_band - m)
            l = (jnp.sum(p_ctx, axis=-1, keepdims=True) + jnp.sum(p_band, axis=-1, keepdims=True)
                 + jnp.exp2(sink - m))
            acc = _dot(p_ctx.astype(bf16), vctx) + _dot(p_band.astype(bf16), vband)
            outs.append(acc / l)
        o_ref[:, j * LANES:(j + 1) * LANES] = jnp.where(lo, outs[0], outs[1]).astype(bf16)


def _window_call(q, kc, vc, k, v, sink, *, tq):
    b, _, n, _ = q.shape
    l_ctx = kc.shape[2]
    return pl.pallas_call(
        _window_kernel,
        name="window_attn",
        grid=(b, n // tq),
        in_specs=[
            pl.BlockSpec((None, 4, tq, LANES), lambda bi, i: (bi, 0, i, 0)),
            pl.BlockSpec((None, 1, l_ctx, LANES), lambda bi, i: (bi, 0, 0, 0)),
            pl.BlockSpec((None, 1, l_ctx, LANES), lambda bi, i: (bi, 0, 0, 0)),
            pl.BlockSpec((None, 1, n, LANES), lambda bi, i: (bi, 0, 0, 0)),
            pl.BlockSpec((None, 1, n, LANES), lambda bi, i: (bi, 0, 0, 0)),
            pl.BlockSpec(memory_space=pltpu.SMEM),
        ],
        out_specs=pl.BlockSpec((None, tq, 4 * LANES), lambda bi, i: (bi, i, 0)),
        out_shape=jax.ShapeDtypeStruct((b, n, 4 * LANES), bf16),
        compiler_params=pltpu.CompilerParams(vmem_limit_bytes=VMEM_LIMIT),
    )(q, kc, vc, k, v, sink)


def _merge_kernel(x_ref, mod_ref, oa_ref, ob_ref, oc_ref, od_ref, wg_ref, wbr_ref, wout_ref, o_ref):
    x = x_ref[...]
    h = _modulate(x, mod_ref[0:1, :], mod_ref[1:2, :]).astype(bf16)
    mix = None
    for k, br_ref in enumerate((oa_ref, ob_ref, oc_ref, od_ref)):
        g = _dot(h, wg_ref[:, k * D_MODEL:(k + 1) * D_MODEL])
        y = _dot(br_ref[...], wbr_ref[k])
        term = _sigmoid(g) * y
        mix = term if mix is None else mix + term
    o_ref[...] = x + mod_ref[2:3, :] * _dot(mix.astype(bf16), wout_ref[...])


def _merge_call(xs, mod, branches, w, tm):
    b, t, d = xs.shape
    tiles = t // tm
    x2 = xs.reshape(b * t, d)
    br = [o.reshape(b * t, BRANCH_WIDTH) for o in branches]
    br_spec = pl.BlockSpec((tm, BRANCH_WIDTH), lambda i: (i, 0))
    out = pl.pallas_call(
        _merge_kernel,
        name="merge",
        grid=(b * tiles,),
        in_specs=[
            pl.BlockSpec((tm, d), lambda i: (i, 0)),
            pl.BlockSpec((None, N_MOD, d), lambda i: (i // tiles, 0, 0)),
            br_spec, br_spec, br_spec, br_spec,
            _const_spec(w["wg"].shape), _const_spec(w["wbr"].shape), _const_spec(w["wout"].shape),
        ],
        out_specs=pl.BlockSpec((tm, d), lambda i: (i, 0)),
        out_shape=jax.ShapeDtypeStruct((b * t, d), f32),
        compiler_params=pltpu.CompilerParams(vmem_limit_bytes=VMEM_LIMIT),
    )(x2, mod, *br, w["wg"], w["wbr"], w["wout"])
    return out.reshape(b, t, d)


def _ffn_kernel(x_ref, xp_ref, xn_ref, mod_ref, wup_ref, cw_ref, cb_ref, wdn_ref, fin_ref, o_ref, *, seq_len,
                final_norm):
    tm = x_ref.shape[0]
    rows = tm + 2 * HALO
    x = x_ref[...]
    xe = jnp.concatenate([xp_ref[...], x, xn_ref[...]], axis=0)
    h = _modulate(xe, mod_ref[3:4, :], mod_ref[4:5, :]).astype(bf16)
    pos = (pl.program_id(0) % (seq_len // tm)) * tm + lax.broadcasted_iota(jnp.int32, (tm, 1), 0)
    first = pos == 0
    last = pos == seq_len - 1

    def conv(u, c0):
        prev = jnp.where(first, 0.0, pltpu.roll(u, 1, 0)[HALO:HALO + tm])
        nxt = jnp.where(last, 0.0, pltpu.roll(u, rows - 1, 0)[HALO:HALO + tm])
        cur = u[HALO:HALO + tm]
        cols = slice(c0, c0 + FFN_CHUNK)
        return prev * cw_ref[0:1, cols] + cur * cw_ref[1:2, cols] + nxt * cw_ref[2:3, cols] + cb_ref[:, cols]

    acc = None
    for j in range(D_FF // FFN_CHUNK):
        cv, cg = j * FFN_CHUNK, D_FF + j * FFN_CHUNK
        val = conv(_dot(h, wup_ref[:, cv:cv + FFN_CHUNK]), cv)
        gate = conv(_dot(h, wup_ref[:, cg:cg + FFN_CHUNK]), cg)
        a = (gate * _sigmoid(gate) * val).astype(bf16)
        part = _dot(a, wdn_ref[cv:cv + FFN_CHUNK, :])
        acc = part if acc is None else acc + part
    out = x + mod_ref[5:6, :] * acc
    if final_norm:
        out = _rms(out) * fin_ref[...]
    o_ref[...] = out


def _ffn_call(xs, mod, w, fin, tm, final_norm):
    b, t, d = xs.shape
    tiles = t // tm
    x2 = xs.reshape(b * t, d)
    hb = tm // HALO
    last_halo = b * t // HALO - 1
    out = pl.pallas_call(
        functools.partial(_ffn_kernel, seq_len=t, final_norm=final_norm),
        name="conv_ffn",
        grid=(b * tiles,),
        in_specs=[
            pl.BlockSpec((tm, d), lambda i: (i, 0)),
            pl.BlockSpec((HALO, d), lambda i: (jnp.maximum(i * hb - 1, 0), 0)),
            pl.BlockSpec((HALO, d), lambda i: (jnp.minimum((i + 1) * hb, last_halo), 0)),
            pl.BlockSpec((None, N_MOD, d), lambda i: (i // tiles, 0, 0)),
            _const_spec(w["wup"].shape), _const_spec(w["cw"].shape), _const_spec(w["cb"].shape),
            _const_spec(w["wdn"].shape), _const_spec((1, d)),
        ],
        out_specs=pl.BlockSpec((tm, d), lambda i: (i, 0)),
        out_shape=jax.ShapeDtypeStruct((b * t, d), f32),
        compiler_params=pltpu.CompilerParams(vmem_limit_bytes=VMEM_LIMIT),
    )(x2, x2, x2, mod, w["wup"], w["cw"], w["cb"], w["wdn"], fin)
    return out.reshape(b, t, d)


def _cols(w, name):
    a, b = _OFF[name]
    return w[:, a:b]


def _pair_heads(w_cols):
    d = w_cols.shape[0]
    return w_cols.reshape(d, GQA_HEADS, HALF)[:, jnp.array(_PAIR_HEAD_ORDER), :].reshape(d, GQA_HEADS * HALF)


def _layer_weights(l, w_in, mla_q_norm, mla_kv_norm, mla_w_uq, mla_w_ukv, gqa_q_norm, gqa_k_norm, w_branch,
                   w_out, ffn_w_up, ffn_conv_w, ffn_conv_b, ffn_w_down):
    wi = w_in[l]
    d = wi.shape[0]
    bkv = _cols(wi, "bkv")
    k_rot = jnp.concatenate([jnp.zeros((d, MLA_NOPE_DIM), f32), bkv[:, MLA_KV_LORA:],
                             jnp.zeros((d, LANES - MLA_NOPE_DIM - MLA_ROPE_DIM), f32)], axis=1)
    w1 = jnp.concatenate([
        _cols(wi, "aq"), _cols(wi, "ak"), _cols(wi, "av"),
        _cols(wi, "bq"), bkv[:, :MLA_KV_LORA], k_rot,
        _pair_heads(_cols(wi, "cq")), _cols(wi, "ck"), _cols(wi, "cv"),
        _pair_heads(_cols(wi, "dq")), _cols(wi, "dk"), _cols(wi, "dv")], axis=1).astype(bf16)
    hq = MLA_NOPE_DIM + MLA_ROPE_DIM
    wuq = mla_w_uq[l].reshape(MLA_Q_LORA, MLA_HEADS, hq)
    wuq = jnp.pad(wuq, ((0, 0), (0, 0), (0, LANES - hq))).reshape(MLA_Q_LORA, MLA_HEADS * LANES).astype(bf16)
    wukv = mla_w_ukv[l].reshape(MLA_KV_LORA, MLA_HEADS, MLA_NOPE_DIM + MLA_V_DIM)
    wk = jnp.pad(wukv[:, :, :MLA_NOPE_DIM], ((0, 0), (0, 0), (0, LANES - MLA_NOPE_DIM)))
    wk = wk.reshape(MLA_KV_LORA, MLA_HEADS * LANES).astype(bf16)
    wv = wukv[:, :, MLA_NOPE_DIM:].reshape(MLA_KV_LORA, MLA_HEADS * MLA_V_DIM).astype(bf16)
    wb = w_branch[l]
    order = jnp.array(_PAIR_HEAD_ORDER)
    paired_rows = lambda m: m.reshape(GQA_HEADS, HALF, -1)[order].reshape(BRANCH_WIDTH, -1)
    wbr = jnp.stack([wb[0], wb[1], paired_rows(wb[2]), paired_rows(wb[3])]).astype(bf16)
    return dict(
        w1=w1, wuq=wuq, wk=wk, wv=wv,
        qn=mla_q_norm[l].reshape(1, -1), kvn=mla_kv_norm[l].reshape(1, -1),
        gq=jnp.tile(gqa_q_norm[l], 2).reshape(1, LANES), gk=jnp.tile(gqa_k_norm[l], 2).reshape(1, LANES),
        wg=_cols(wi, "g").astype(bf16), wbr=wbr, wout=w_out[l].astype(bf16),
        wup=ffn_w_up[l].astype(bf16), cw=ffn_conv_w[l], cb=ffn_conv_b[l].reshape(1, -1),
        wdn=ffn_w_down[l].astype(bf16))


def _rope_tables(n, l_ctx):
    t = jnp.arange(n, dtype=jnp.int32)
    row, col = (t // GRID_W).astype(f32), (t % GRID_W).astype(f32)

    def expanded(rot_dim):
        axis_dim = rot_dim // 2
        inv = ROPE_THETA ** (-jnp.arange(0, axis_dim, 2, dtype=f32) / axis_dim)
        ang = jnp.concatenate([row[:, None] * inv, col[:, None] * inv], axis=-1)
        cos = jnp.repeat(jnp.cos(ang), 2, axis=-1)
        sin = jnp.repeat(jnp.sin(ang), 2, axis=-1) * jnp.tile(jnp.array([-1.0, 1.0], f32), rot_dim // 2)
        return cos, sin

    c64, s64 = expanded(2 * HALF // 2)
    c64, s64 = jnp.tile(c64, (1, 2)), jnp.tile(s64, (1, 2))
    c32, s32 = expanded(MLA_ROPE_DIM)
    pad = LANES - MLA_NOPE_DIM - MLA_ROPE_DIM
    cb = jnp.concatenate([jnp.ones((n, MLA_NOPE_DIM), f32), c32, jnp.ones((n, pad), f32)], axis=1)
    sb = jnp.concatenate([jnp.zeros((n, MLA_NOPE_DIM), f32), s32, jnp.zeros((n, pad), f32)], axis=1)
    ones, zeros = jnp.ones((l_ctx, LANES), f32), jnp.zeros((l_ctx, LANES), f32)
    return (c64, s64, cb, sb), (ones, zeros, ones, zeros)


def kernel(x, c, ctx, c_ctx, w_mod, b_mod, w_in, diff_lambda, diff_subln, mla_q_norm, mla_kv_norm, mla_w_uq,
           mla_w_ukv, gqa_q_norm, gqa_k_norm, swa_sink, w_branch, w_out, ffn_w_up, ffn_conv_w, ffn_conv_b,
           ffn_w_down, final_norm):
    b, n, d = x.shape
    l_ctx = ctx.shape[1]
    depth = w_mod.shape[0]
    tm_x = min(512, n)
    tq = min(256, n)
    tk = min(512, n // 2)

    cs = jnp.zeros((8, d), f32).at[:b].set(c).at[b].set(c_ctx)
    mods = _mod_call(cs, w_mod, b_mod).reshape(depth, 8, N_MOD, d)
    tab_x, tab_c = _rope_tables(n, l_ctx)
    fin = final_norm.reshape(1, d)

    xs, xc = x, ctx
    for l in range(depth):
        w = _layer_weights(l, w_in, mla_q_norm, mla_kv_norm, mla_w_uq, mla_w_ukv, gqa_q_norm, gqa_k_norm,
                           w_branch, w_out, ffn_w_up, ffn_conv_w, ffn_conv_b, ffn_w_down)
        mod_x = mods[l, :b]
        mod_c = jnp.broadcast_to(mods[l, b][None], (b, N_MOD, d))
        lam_init = 0.8 - 0.6 * math.exp(-0.3 * l)
        a_extras = (diff_lambda[l], diff_subln[l].reshape(1, LANES))
        sink = swa_sink[l]
        update_ctx = l < depth - 1

        px = _proj_call(xs, mod_x, tab_x, w, tm_x)
        pc = _proj_call(xc, mod_c, tab_c, w, l_ctx)
        qa, ka, va, qb, kb, vb, qc, kc, vc, qd, kd, vd = px
        qa_c, ka_c, va_c, qb_c, kb_c, vb_c, qc_c, kc_c, vc_c, qd_c, kd_c, vd_c = pc

        oa = _flash_call("A", qa, ka_c, va_c, ka, va, tq=tq, tk=tk, extras=a_extras, lam_init=lam_init)
        ob = _flash_call("B", qb, kb_c, vb_c, kb, vb, tq=tq, tk=tk)
        oc = _flash_call("C", qc, kc_c, vc_c, kc, vc, tq=tq, tk=tk)
        od = _window_call(qd, kd_c, vd_c, kd, vd, sink, tq=tq)
        xs_mid = _merge_call(xs, mod_x, (oa, ob, oc, od), w, tm_x)
        xs_new = _ffn_call(xs_mid, mod_x, w, fin, tm_x, final_norm=not update_ctx)

        if update_ctx:
            oa = _flash_call("A", qa_c, ka_c, va_c, None, None, tq=l_ctx, tk=tk, extras=a_extras,
                             lam_init=lam_init)
            ob = _flash_call("B", qb_c, kb_c, vb_c, None, None, tq=l_ctx, tk=tk)
            oc = _flash_call("C", qc_c, kc_c, vc_c, None, None, tq=l_ctx, tk=tk)
            od = _flash_call("C", qd_c, kd_c, vd_c, None, None, tq=l_ctx, tk=tk, sink=sink)
            xc_mid = _merge_call(xc, mod_c, (oa, ob, oc, od), w, l_ctx)
            xc = _ffn_call(xc_mid, mod_c, w, fin, l_ctx, final_norm=False)
        xs = xs_new
    return xs
```

```python
import functools
import math

import jax
import jax.numpy as jnp
from jax import lax
from jax.experimental import pallas as pl
from jax.experimental.pallas import tpu as pltpu

f32 = jnp.float32
bf16 = jnp.bfloat16

LANES = 128
HALF = 64
D_MODEL = 1024
GRID_W = 64
WINDOW = 128
N_BRANCH = 4
N_MOD = 6
EPS = 1e-6
ROPE_THETA = 10000.0
NEG_INF = -1e30
LOG2E = math.log2(math.e)
DIFF_HEADS = 4
DIFF_QK_DIM = 64
MLA_HEADS = 8
MLA_Q_LORA = 256
MLA_KV_LORA = 256
MLA_NOPE_DIM = 64
MLA_ROPE_DIM = 32
MLA_V_DIM = 64
GQA_HEADS = 8
BRANCH_WIDTH = 512
D_FF = 2816
FFN_CHUNK = 256
HALO = 8
VMEM_LIMIT = 56 * 1024 * 1024

_OFF = {}
_o = 0
for _name, _w in (("aq", 512), ("ak", 512), ("av", 512), ("bq", 256), ("bkv", 288), ("cq", 512), ("ck", 128),
                  ("cv", 128), ("dq", 512), ("dk", 128), ("dv", 128), ("g", 4096)):
    _OFF[_name] = (_o, _o + _w)
    _o += _w
_PAIR_HEAD_ORDER = (0, 4, 1, 5, 2, 6, 3, 7)


def _dot(a, b):
    return jnp.dot(a, b, preferred_element_type=f32)


def _dot_nt(a, b):
    return lax.dot_general(a, b, (((1,), (1,)), ((), ())), preferred_element_type=f32)


def _sigmoid(v):
    return 1.0 / (1.0 + jnp.exp(-v))


def _rms(v):
    return v * lax.rsqrt(jnp.mean(v * v, axis=-1, keepdims=True) + EPS)


def _modulate(v, shift, scale):
    return _rms(v) * (1.0 + scale) + shift


def _const_spec(shape):
    zeros = (0,) * len(shape)
    return pl.BlockSpec(shape, lambda *_: zeros, pipeline_mode=pl.Buffered(1))


def _mod_kernel(c_ref, w_ref, b_ref, o_ref):
    cv = c_ref[...]
    a = cv * _sigmoid(cv)
    o_ref[...] = _dot(a, w_ref[...]) + b_ref[...]


def _mod_call(cs, w_mod, b_mod):
    depth, d, width = w_mod.shape
    tn = 1536
    return pl.pallas_call(
        _mod_kernel,
        name="mod_vectors",
        grid=(depth, width // tn),
        in_specs=[
            pl.BlockSpec((8, d), lambda l, j: (0, 0)),
            pl.BlockSpec((None, d, tn), lambda l, j: (l, 0, j)),
            pl.BlockSpec((None, 1, tn), lambda l, j: (l, 0, j)),
        ],
        out_specs=pl.BlockSpec((None, 8, tn), lambda l, j: (l, 0, j)),
        out_shape=jax.ShapeDtypeStruct((depth, 8, width), f32),
        compiler_params=pltpu.CompilerParams(vmem_limit_bytes=VMEM_LIMIT),
    )(cs, w_mod, b_mod.reshape(depth, 1, width))


def _proj_kernel(x_ref, mod_ref, c64_ref, s64_ref, cb_ref, sb_ref, w1_ref, qn_ref, kvn_ref, wuq_ref, wk_ref,
                 wv_ref, gq_ref, gk_ref,
                 qa_ref, ka_ref, va_ref, qb_ref, kb_ref, vb_ref, qc_ref, kc_ref, vc_ref, qd_ref, kd_ref, vd_ref,
                 *, scale_b):
    tm = x_ref.shape[0]
    h = _modulate(x_ref[...], mod_ref[0:1, :], mod_ref[1:2, :]).astype(bf16)
    lane = lax.broadcasted_iota(jnp.int32, (tm, LANES), 1)
    even = (lane & 1) == 0
    lo = lane < HALF
    c64, s64, cb, sb = c64_ref[...], s64_ref[...], cb_ref[...], sb_ref[...]

    def rope(v, cos, sin):
        partner = jnp.where(even, pltpu.roll(v, LANES - 1, 1), pltpu.roll(v, 1, 1))
        return v * cos + partner * sin

    def head_norm(v, gain):
        sq = v * v
        s_lo = jnp.sum(jnp.where(lo, sq, 0.0), axis=-1, keepdims=True)
        s_hi = jnp.sum(jnp.where(lo, 0.0, sq), axis=-1, keepdims=True)
        inv = jnp.where(lo, lax.rsqrt(s_lo * (1.0 / HALF) + EPS), lax.rsqrt(s_hi * (1.0 / HALF) + EPS))
        return v * inv * gain

    def chunk(z, j):
        return z[:, j * LANES:(j + 1) * LANES]

    qk_scale = DIFF_QK_DIM ** -0.5 * LOG2E

    z = _dot(h, w1_ref[:, 0:1536])
    for j in range(4):
        qa_ref[j] = (rope(chunk(z, j), c64, s64) * qk_scale).astype(bf16)
        ka_ref[j] = rope(chunk(z, 4 + j), c64, s64).astype(bf16)
        va_ref[j] = chunk(z, 8 + j).astype(bf16)

    z = _dot(h, w1_ref[:, 1536:2176])
    dq = (_rms(z[:, 0:256]) * qn_ref[...]).astype(bf16)
    ckv = (_rms(z[:, 256:512]) * kvn_ref[...]).astype(bf16)
    k_rot = rope(z[:, 512:640], cb, sb)
    zq = _dot(dq, wuq_ref[...])
    zk = _dot(ckv, wk_ref[...])
    zv = _dot(ckv, wv_ref[...])
    for j in range(MLA_HEADS):
        qb_ref[j] = (rope(chunk(zq, j), cb, sb) * scale_b).astype(bf16)
        kb_ref[j] = (chunk(zk, j) + k_rot).astype(bf16)
    for j in range(4):
        vb_ref[j] = chunk(zv, j).astype(bf16)

    z = _dot(h, w1_ref[:, 2176:2944])
    gq, gk = gq_ref[...], gk_ref[...]
    for j in range(4):
        qc_ref[j] = (rope(head_norm(chunk(z, j), gq), c64, s64) * qk_scale).astype(bf16)
    kc_ref[0] = rope(head_norm(chunk(z, 4), gk), c64, s64).astype(bf16)
    vc_ref[0] = chunk(z, 5).astype(bf16)

    z = _dot(h, w1_ref[:, 2944:3712])
    for j in range(4):
        qd_ref[j] = (rope(chunk(z, j), c64, s64) * qk_scale).astype(bf16)
    kd_ref[0] = rope(chunk(z, 4), c64, s64).astype(bf16)
    vd_ref[0] = chunk(z, 5).astype(bf16)


def _proj_call(xs, mod, tables, w, tm):
    b, t, d = xs.shape
    c64, s64, cb, sb = tables
    grid = (b, t // tm)

    def out(nchunks):
        return (jax.ShapeDtypeStruct((b, nchunks, t, LANES), bf16),
                pl.BlockSpec((None, nchunks, tm, LANES), lambda bi, i: (bi, 0, i, 0)))

    outs = [out(n) for n in (4, 4, 4, 8, 8, 4, 4, 1, 1, 4, 1, 1)]
    tab_spec = pl.BlockSpec((tm, LANES), lambda bi, i: (i, 0))
    scale_b = (MLA_NOPE_DIM + MLA_ROPE_DIM) ** -0.5 * LOG2E
    return pl.pallas_call(
        functools.partial(_proj_kernel, scale_b=scale_b),
        name="in_proj",
        grid=grid,
        in_specs=[
            pl.BlockSpec((None, tm, d), lambda bi, i: (bi, i, 0)),
            pl.BlockSpec((None, N_MOD, d), lambda bi, i: (bi, 0, 0)),
            tab_spec, tab_spec, tab_spec, tab_spec,
            _const_spec(w["w1"].shape), _const_spec((1, 256)), _const_spec((1, 256)),
            _const_spec(w["wuq"].shape), _const_spec(w["wk"].shape), _const_spec(w["wv"].shape),
            _const_spec((1, LANES)), _const_spec((1, LANES)),
        ],
        out_specs=[o[1] for o in outs],
        out_shape=[o[0] for o in outs],
        compiler_params=pltpu.CompilerParams(vmem_limit_bytes=VMEM_LIMIT),
    )(xs, mod, c64, s64, cb, sb, w["w1"], w["qn"], w["kvn"], w["wuq"], w["wk"], w["wv"], w["gq"], w["gk"])


def _flash_kernel(*refs, mode, has_x, has_sink, tk, lam_init):
    refs = list(refs)
    q_ref, kc_ref, vc_ref = refs[:3]
    pos = 3
    if has_x:
        k_ref, v_ref = refs[pos:pos + 2]
        pos += 2
    if mode == "A":
        dl_ref, subln_ref = refs[pos:pos + 2]
        pos += 2
    if has_sink:
        sink_ref = refs[pos]
        pos += 1
    o_ref, q2_ref, m_ref, acc_ref, sc_ref, pc_ref, alc_ref = refs[pos:pos + 7]
    if has_x:
        sa_ref, sb_ref, pa_ref, pb_ref, ala_ref, alb_ref = refs[pos + 7:pos + 13]

    tq = q_ref.shape[1]
    lane = lax.broadcasted_iota(jnp.int32, (tq, LANES), 1)
    lo = lane < HALF
    shared_k = mode != "B"
    if shared_k:
        q = q_ref[0]
        zero = jnp.zeros_like(q)
        q2_ref[0:tq, :] = jnp.where(lo, q, zero)
        q2_ref[tq:2 * tq, :] = jnp.where(lo, zero, q)
    else:
        q2_ref[0:tq, :] = q_ref[0]
        q2_ref[tq:2 * tq, :] = q_ref[1]

    for u in range(2):
        if has_sink:
            sink = sink_ref[pl.program_id(1) + 4 * u] * LOG2E
            m_ref[u] = jnp.full((tq, LANES), sink, f32)
            acc_ref[u, :, LANES:] = jnp.ones((tq, LANES), f32)
        else:
            m_ref[u] = jnp.full((tq, LANES), NEG_INF, f32)
            acc_ref[u, :, LANES:] = jnp.zeros((tq, LANES), f32)
        acc_ref[u, :, :LANES] = jnp.zeros((tq, LANES), f32)

    def scores(kref, start, size, s_ref):
        if shared_k:
            s_ref[...] = _dot_nt(q2_ref[...], kref[0, pl.ds(start, size), :])
        else:
            for u in range(2):
                rows = slice(u * tq, (u + 1) * tq)
                s_ref[rows, :] = _dot_nt(q2_ref[rows, :], kref[u, pl.ds(start, size), :])

    def softmax(s_ref, p_ref, al_ref):
        for u in range(2):
            rows = slice(u * tq, (u + 1) * tq)
            cols = [s_ref[rows, j * LANES:(j + 1) * LANES] for j in range(s_ref.shape[1] // LANES)]
            m_prev = m_ref[u]
            m_new = jnp.maximum(m_prev, jnp.max(functools.reduce(jnp.maximum, cols), axis=-1, keepdims=True))
            al_ref[u] = jnp.exp2(m_prev - m_new)
            for j, cj in enumerate(cols):
                p_ref[rows, j * LANES:(j + 1) * LANES] = jnp.exp2(cj - m_new).astype(bf16)
            m_ref[u] = m_new

    def pv(p_ref, al_ref, vblk):
        ones = jnp.ones((vblk.shape[0], LANES), bf16)
        vext = jnp.concatenate([vblk, ones], axis=1)
        for u in range(2):
            alpha = al_ref[u]
            alpha2 = jnp.concatenate([alpha, alpha], axis=1)
            acc_ref[u] = alpha2 * acc_ref[u] + _dot(p_ref[u * tq:(u + 1) * tq, :], vext)

    def keys(c):
        return pl.ds(pl.multiple_of(c * tk, tk), tk)

    scores(kc_ref, 0, kc_ref.shape[1], sc_ref)
    if has_x:
        n_chunks = k_ref.shape[1] // tk
        scores(k_ref, 0, tk, sa_ref)
    softmax(sc_ref, pc_ref, alc_ref)
    if not has_x:
        pv(pc_ref, alc_ref, vc_ref[0])
    else:
        scores(k_ref, tk, tk, sb_ref)
        softmax(sa_ref, pa_ref, ala_ref)
        pv(pc_ref, alc_ref, vc_ref[0])

        def body(cc, carry):
            c = 1 + 2 * cc
            scores(k_ref, pl.multiple_of((c + 1) * tk, tk), tk, sa_ref)
            softmax(sb_ref, pb_ref, alb_ref)
            pv(pa_ref, ala_ref, v_ref[0, keys(c - 1), :])
            scores(k_ref, pl.multiple_of((c + 2) * tk, tk), tk, sb_ref)
            softmax(sa_ref, pa_ref, ala_ref)
            pv(pb_ref, alb_ref, v_ref[0, keys(c), :])
            return carry

        lax.fori_loop(0, (n_chunks - 2) // 2, body, 0, unroll=True)
        softmax(sb_ref, pb_ref, alb_ref)
        pv(pa_ref, ala_ref, v_ref[0, keys(n_chunks - 2), :])
        pv(pb_ref, alb_ref, v_ref[0, keys(n_chunks - 1), :])

    o0 = acc_ref[0, :, :LANES] / acc_ref[0, :, LANES:]
    o1 = acc_ref[1, :, :LANES] / acc_ref[1, :, LANES:]
    if mode == "A":
        dl = dl_ref[...]
        lam = (jnp.exp(jnp.sum(dl[0:1] * dl[1:2], axis=-1, keepdims=True))
               - jnp.exp(jnp.sum(dl[2:3] * dl[3:4], axis=-1, keepdims=True)) + lam_init)
        o = _rms(o0 - lam * o1) * subln_ref[...] * (1.0 - lam_init)
    else:
        o = jnp.where(lo, o0, o1)
    o_ref[...] = o.astype(bf16)


def _flash_call(mode, q, kc, vc, k, v, *, tq, tk, extras=(), sink=None, lam_init=0.0):
    b, _, t, _ = q.shape
    l_ctx = kc.shape[2]
    has_x = k is not None
    gq = 2 if mode == "B" else 1
    shared_kv = mode == "C"

    def kv_idx(bi, g, i):
        return (bi, 0 if shared_kv else g, 0, 0)

    in_specs = [
        pl.BlockSpec((None, gq, tq, LANES), lambda bi, g, i: (bi, g, i, 0)),
        pl.BlockSpec((None, gq, l_ctx, LANES), kv_idx),
        pl.BlockSpec((None, 1, l_ctx, LANES), kv_idx),
    ]
    args = [q, kc, vc]
    if has_x:
        n = k.shape[2]
        in_specs += [pl.BlockSpec((None, gq, n, LANES), kv_idx), pl.BlockSpec((None, 1, n, LANES), kv_idx)]
        args += [k, v]
    for e in extras:
        in_specs.append(pl.BlockSpec(e.shape, lambda bi, g, i, nd=e.ndim: (0,) * nd))
        args.append(e)
    if sink is not None:
        in_specs.append(pl.BlockSpec(memory_space=pltpu.SMEM))
        args.append(sink)
    state = pltpu.VMEM((2, tq, LANES), f32)
    scratch = [pltpu.VMEM((2 * tq, LANES), bf16), state, pltpu.VMEM((2, tq, 2 * LANES), f32),
               pltpu.VMEM((2 * tq, l_ctx), f32), pltpu.VMEM((2 * tq, l_ctx), bf16), state]
    if has_x:
        assert (k.shape[2] // tk) % 2 == 0, "key chunks are consumed two per loop trip"
        s_buf, p_buf = pltpu.VMEM((2 * tq, tk), f32), pltpu.VMEM((2 * tq, tk), bf16)
        scratch += [s_buf, s_buf, p_buf, p_buf, state, state]
    return pl.pallas_call(
        functools.partial(_flash_kernel, mode=mode, has_x=has_x, has_sink=sink is not None, tk=tk,
                          lam_init=lam_init),
        name="flash_" + mode + ("" if has_x else "_ctx"),
        grid=(b, 4, t // tq),
        in_specs=in_specs,
        out_specs=pl.BlockSpec((None, tq, LANES), lambda bi, g, i: (bi, i, g)),
        out_shape=jax.ShapeDtypeStruct((b, t, 4 * LANES), bf16),
        scratch_shapes=scratch,
        compiler_params=pltpu.CompilerParams(vmem_limit_bytes=VMEM_LIMIT),
    )(*args)


def _window_kernel(q_ref, kc_ref, vc_ref, k_ref, v_ref, sink_ref, o_ref):
    tq = q_ref.shape[1]
    n = k_ref.shape[1]
    band = tq + 2 * WINDOW
    t0 = pl.program_id(1) * tq
    start = pl.multiple_of(jnp.clip(t0 - WINDOW, 0, n - band), WINDOW)
    kband = k_ref[0, pl.ds(start, band), :]
    vband = v_ref[0, pl.ds(start, band), :]
    kctx, vctx = kc_ref[0], vc_ref[0]
    kpos = start + lax.broadcasted_iota(jnp.int32, (tq, band), 1)
    qpos = t0 + lax.broadcasted_iota(jnp.int32, (tq, band), 0)
    valid = jnp.abs(kpos - qpos) <= WINDOW
    lane = lax.broadcasted_iota(jnp.int32, (tq, LANES), 1)
    lo = lane < HALF
    for j in range(4):
        q = q_ref[j]
        zero = jnp.zeros_like(q)
        outs = []
        for u in range(2):
            qv = jnp.where(lo, q, zero) if u == 0 else jnp.where(lo, zero, q)
            sink = sink_ref[j + 4 * u] * LOG2E
            s_ctx = _dot_nt(qv, kctx)
            s_band = jnp.where(valid, _dot_nt(qv, kband), NEG_INF)
            m = jnp.maximum(jnp.maximum(jnp.max(s_ctx, axis=-1, keepdims=True),
                                        jnp.max(s_band, axis=-1, keepdims=True)), sink)
            p_ctx = jnp.exp2(s_ctx - m)
            p_band = jnp.exp2(s_band - m)
            l = (jnp.sum(p_ctx, axis=-1, keepdims=True) + jnp.sum(p_band, axis=-1, keepdims=True)
                 + jnp.exp2(sink - m))
            acc = _dot(p_ctx.astype(bf16), vctx) + _dot(p_band.astype(bf16), vband)
            outs.append(acc / l)
        o_ref[:, j * LANES:(j + 1) * LANES] = jnp.where(lo, outs[0], outs[1]).astype(bf16)


def _window_call(q, kc, vc, k, v, sink, *, tq):
    b, _, n, _ = q.shape
    l_ctx = kc.shape[2]
    return pl.pallas_call(
        _window_kernel,
        name="window_attn",
        grid=(b, n // tq),
        in_specs=[
            pl.BlockSpec((None, 4, tq, LANES), lambda bi, i: (bi, 0, i, 0)),
            pl.BlockSpec((None, 1, l_ctx, LANES), lambda bi, i: (bi, 0, 0, 0)),
            pl.BlockSpec((None, 1, l_ctx, LANES), lambda bi, i: (bi, 0, 0, 0)),
            pl.BlockSpec((None, 1, n, LANES), lambda bi, i: (bi, 0, 0, 0)),
            pl.BlockSpec((None, 1, n, LANES), lambda bi, i: (bi, 0, 0, 0)),
            pl.BlockSpec(memory_space=pltpu.SMEM),
        ],
        out_specs=pl.BlockSpec((None, tq, 4 * LANES), lambda bi, i: (bi, i, 0)),
        out_shape=jax.ShapeDtypeStruct((b, n, 4 * LANES), bf16),
        compiler_params=pltpu.CompilerParams(vmem_limit_bytes=VMEM_LIMIT),
    )(q, kc, vc, k, v, sink)


def _merge_kernel(x_ref, mod_ref, oa_ref, ob_ref, oc_ref, od_ref, wg_ref, wbr_ref, wout_ref, o_ref):
    x = x_ref[...]
    h = _modulate(x, mod_ref[0:1, :], mod_ref[1:2, :]).astype(bf16)
    mix = None
    for k, br_ref in enumerate((oa_ref, ob_ref, oc_ref, od_ref)):
        g = _dot(h, wg_ref[:, k * D_MODEL:(k + 1) * D_MODEL])
        y = _dot(br_ref[...], wbr_ref[k])
        term = _sigmoid(g) * y
        mix = term if mix is None else mix + term
    o_ref[...] = x + mod_ref[2:3, :] * _dot(mix.astype(bf16), wout_ref[...])


def _merge_call(xs, mod, branches, w, tm):
    b, t, d = xs.shape
    tiles = t // tm
    x2 = xs.reshape(b * t, d)
    br = [o.reshape(b * t, BRANCH_WIDTH) for o in branches]
    br_spec = pl.BlockSpec((tm, BRANCH_WIDTH), lambda i: (i, 0))
    out = pl.pallas_call(
        _merge_kernel,
        name="merge",
        grid=(b * tiles,),
        in_specs=[
            pl.BlockSpec((tm, d), lambda i: (i, 0)),
            pl.BlockSpec((None, N_MOD, d), lambda i: (i // tiles, 0, 0)),
            br_spec, br_spec, br_spec, br_spec,
            _const_spec(w["wg"].shape), _const_spec(w["wbr"].shape), _const_spec(w["wout"].shape),
        ],
        out_specs=pl.BlockSpec((tm, d), lambda i: (i, 0)),
        out_shape=jax.ShapeDtypeStruct((b * t, d), f32),
        compiler_params=pltpu.CompilerParams(vmem_limit_bytes=VMEM_LIMIT),
    )(x2, mod, *br, w["wg"], w["wbr"], w["wout"])
    return out.reshape(b, t, d)


def _ffn_kernel(x_ref, xp_ref, xn_ref, mod_ref, wup_ref, cw_ref, cb_ref, wdn_ref, fin_ref, o_ref, u_ref, *,
                seq_len, final_norm):
    tm = x_ref.shape[0]
    x = x_ref[...]
    shift, scale = mod_ref[3:4, :], mod_ref[4:5, :]
    tile = pl.program_id(0) % (seq_len // tm)
    keep_prev = (tile != 0).astype(f32)
    keep_next = (tile != seq_len // tm - 1).astype(f32)
    h = jnp.concatenate([
        (_modulate(xp_ref[...], shift, scale) * keep_prev).astype(bf16),
        _modulate(x, shift, scale).astype(bf16),
        (_modulate(xn_ref[...], shift, scale) * keep_next).astype(bf16)], axis=0)

    def conv(c0):
        cols = slice(c0, c0 + FFN_CHUNK)
        u_ref[:, cols] = _dot(h, wup_ref[:, cols])
        prev = u_ref[HALO - 1:HALO - 1 + tm, cols]
        cur = u_ref[HALO:HALO + tm, cols]
        nxt = u_ref[HALO + 1:HALO + 1 + tm, cols]
        return prev * cw_ref[0:1, cols] + cur * cw_ref[1:2, cols] + nxt * cw_ref[2:3, cols] + cb_ref[:, cols]

    acc = None
    for j in range(D_FF // FFN_CHUNK):
        cv, cg = j * FFN_CHUNK, D_FF + j * FFN_CHUNK
        val = conv(cv)
        gate = conv(cg)
        a = (gate * _sigmoid(gate) * val).astype(bf16)
        part = _dot(a, wdn_ref[cv:cv + FFN_CHUNK, :])
        acc = part if acc is None else acc + part
    out = x + mod_ref[5:6, :] * acc
    if final_norm:
        out = _rms(out) * fin_ref[...]
    o_ref[...] = out


def _ffn_call(xs, mod, w, fin, tm, final_norm):
    b, t, d = xs.shape
    tiles = t // tm
    x2 = xs.reshape(b * t, d)
    hb = tm // HALO
    last_halo = b * t // HALO - 1
    out = pl.pallas_call(
        functools.partial(_ffn_kernel, seq_len=t, final_norm=final_norm),
        name="conv_ffn",
        grid=(b * tiles,),
        in_specs=[
            pl.BlockSpec((tm, d), lambda i: (i, 0)),
            pl.BlockSpec((HALO, d), lambda i: (jnp.maximum(i * hb - 1, 0), 0)),
            pl.BlockSpec((HALO, d), lambda i: (jnp.minimum((i + 1) * hb, last_halo), 0)),
            pl.BlockSpec((None, N_MOD, d), lambda i: (i // tiles, 0, 0)),
            _const_spec(w["wup"].shape), _const_spec(w["cw"].shape), _const_spec(w["cb"].shape),
            _const_spec(w["wdn"].shape), _const_spec((1, d)),
        ],
        out_specs=pl.BlockSpec((tm, d), lambda i: (i, 0)),
        out_shape=jax.ShapeDtypeStruct((b * t, d), f32),
        scratch_shapes=[pltpu.VMEM((tm + 2 * HALO, 2 * D_FF), f32)],
        compiler_params=pltpu.CompilerParams(vmem_limit_bytes=VMEM_LIMIT),
    )(x2, x2, x2, mod, w["wup"], w["cw"], w["cb"], w["wdn"], fin)
    return out.reshape(b, t, d)


def _cols(w, name):
    a, b = _OFF[name]
    return w[:, a:b]


def _pair_heads(w_cols):
    d = w_cols.shape[0]
    return w_cols.reshape(d, GQA_HEADS, HALF)[:, jnp.array(_PAIR_HEAD_ORDER), :].reshape(d, GQA_HEADS * HALF)


def _layer_weights(l, w_in, mla_q_norm, mla_kv_norm, mla_w_uq, mla_w_ukv, gqa_q_norm, gqa_k_norm, w_branch,
                   w_out, ffn_w_up, ffn_conv_w, ffn_conv_b, ffn_w_down):
    wi = w_in[l]
    d = wi.shape[0]
    bkv = _cols(wi, "bkv")
    k_rot = jnp.concatenate([jnp.zeros((d, MLA_NOPE_DIM), f32), bkv[:, MLA_KV_LORA:],
                             jnp.zeros((d, LANES - MLA_NOPE_DIM - MLA_ROPE_DIM), f32)], axis=1)
    w1 = jnp.concatenate([
        _cols(wi, "aq"), _cols(wi, "ak"), _cols(wi, "av"),
        _cols(wi, "bq"), bkv[:, :MLA_KV_LORA], k_rot,
        _pair_heads(_cols(wi, "cq")), _cols(wi, "ck"), _cols(wi, "cv"),
        _pair_heads(_cols(wi, "dq")), _cols(wi, "dk"), _cols(wi, "dv")], axis=1).astype(bf16)
    hq = MLA_NOPE_DIM + MLA_ROPE_DIM
    wuq = mla_w_uq[l].reshape(MLA_Q_LORA, MLA_HEADS, hq)
    wuq = jnp.pad(wuq, ((0, 0), (0, 0), (0, LANES - hq))).reshape(MLA_Q_LORA, MLA_HEADS * LANES).astype(bf16)
    wukv = mla_w_ukv[l].reshape(MLA_KV_LORA, MLA_HEADS, MLA_NOPE_DIM + MLA_V_DIM)
    wk = jnp.pad(wukv[:, :, :MLA_NOPE_DIM], ((0, 0), (0, 0), (0, LANES - MLA_NOPE_DIM)))
    wk = wk.reshape(MLA_KV_LORA, MLA_HEADS * LANES).astype(bf16)
    wv = wukv[:, :, MLA_NOPE_DIM:].reshape(MLA_KV_LORA, MLA_HEADS * MLA_V_DIM).astype(bf16)
    wb = w_branch[l]
    order = jnp.array(_PAIR_HEAD_ORDER)
    paired_rows = lambda m: m.reshape(GQA_HEADS, HALF, -1)[order].reshape(BRANCH_WIDTH, -1)
    wbr = jnp.stack([wb[0], wb[1], paired_rows(wb[2]), paired_rows(wb[3])]).astype(bf16)
    return dict(
        w1=w1, wuq=wuq, wk=wk, wv=wv,
        qn=mla_q_norm[l].reshape(1, -1), kvn=mla_kv_norm[l].reshape(1, -1),
        gq=jnp.tile(gqa_q_norm[l], 2).reshape(1, LANES), gk=jnp.tile(gqa_k_norm[l], 2).reshape(1, LANES),
        wg=_cols(wi, "g").astype(bf16), wbr=wbr, wout=w_out[l].astype(bf16),
        wup=ffn_w_up[l].astype(bf16), cw=ffn_conv_w[l], cb=ffn_conv_b[l].reshape(1, -1),
        wdn=ffn_w_down[l].astype(bf16))


def _rope_tables(n, l_ctx):
    t = jnp.arange(n, dtype=jnp.int32)
    row, col = (t // GRID_W).astype(f32), (t % GRID_W).astype(f32)

    def expanded(rot_dim):
        axis_dim = rot_dim // 2
        inv = ROPE_THETA ** (-jnp.arange(0, axis_dim, 2, dtype=f32) / axis_dim)
        ang = jnp.concatenate([row[:, None] * inv, col[:, None] * inv], axis=-1)
        cos = jnp.repeat(jnp.cos(ang), 2, axis=-1)
        sin = jnp.repeat(jnp.sin(ang), 2, axis=-1) * jnp.tile(jnp.array([-1.0, 1.0], f32), rot_dim // 2)
        return cos, sin

    c64, s64 = expanded(2 * HALF // 2)
    c64, s64 = jnp.tile(c64, (1, 2)), jnp.tile(s64, (1, 2))
    c32, s32 = expanded(MLA_ROPE_DIM)
    pad = LANES - MLA_NOPE_DIM - MLA_ROPE_DIM
    cb = jnp.concatenate([jnp.ones((n, MLA_NOPE_DIM), f32), c32, jnp.ones((n, pad), f32)], axis=1)
    sb = jnp.concatenate([jnp.zeros((n, MLA_NOPE_DIM), f32), s32, jnp.zeros((n, pad), f32)], axis=1)
    ones, zeros = jnp.ones((l_ctx, LANES), f32), jnp.zeros((l_ctx, LANES), f32)
    return (c64, s64, cb, sb), (ones, zeros, ones, zeros)


def kernel(x, c, ctx, c_ctx, w_mod, b_mod, w_in, diff_lambda, diff_subln, mla_q_norm, mla_kv_norm, mla_w_uq,
           mla_w_ukv, gqa_q_norm, gqa_k_norm, swa_sink, w_branch, w_out, ffn_w_up, ffn_conv_w, ffn_conv_b,
           ffn_w_down, final_norm):
    b, n, d = x.shape
    l_ctx = ctx.shape[1]
    depth = w_mod.shape[0]
    tm_x = min(512, n)
    tq = min(256, n)
    tk = min(1024, n // 2)

    cs = jnp.zeros((8, d), f32).at[:b].set(c).at[b].set(c_ctx)
    mods = _mod_call(cs, w_mod, b_mod).reshape(depth, 8, N_MOD, d)
    tab_x, tab_c = _rope_tables(n, l_ctx)
    fin = final_norm.reshape(1, d)

    xs, xc = x, ctx
    for l in range(depth):
        w = _layer_weights(l, w_in, mla_q_norm, mla_kv_norm, mla_w_uq, mla_w_ukv, gqa_q_norm, gqa_k_norm,
                           w_branch, w_out, ffn_w_up, ffn_conv_w, ffn_conv_b, ffn_w_down)
        mod_x = mods[l, :b]
        mod_c = jnp.broadcast_to(mods[l, b][None], (b, N_MOD, d))
        lam_init = 0.8 - 0.6 * math.exp(-0.3 * l)
        a_extras = (diff_lambda[l], diff_subln[l].reshape(1, LANES))
        sink = swa_sink[l]
        update_ctx = l < depth - 1

        px = _proj_call(xs, mod_x, tab_x, w, tm_x)
        pc = _proj_call(xc, mod_c, tab_c, w, l_ctx)
        qa, ka, va, qb, kb, vb, qc, kc, vc, qd, kd, vd = px
        qa_c, ka_c, va_c, qb_c, kb_c, vb_c, qc_c, kc_c, vc_c, qd_c, kd_c, vd_c = pc

        oa = _flash_call("A", qa, ka_c, va_c, ka, va, tq=tq, tk=tk, extras=a_extras, lam_init=lam_init)
        ob = _flash_call("B", qb, kb_c, vb_c, kb, vb, tq=tq, tk=tk)
        oc = _flash_call("C", qc, kc_c, vc_c, kc, vc, tq=tq, tk=tk)
        od = _window_call(qd, kd_c, vd_c, kd, vd, sink, tq=min(256, n))
        xs_mid = _merge_call(xs, mod_x, (oa, ob, oc, od), w, tm_x)
        xs_new = _ffn_call(xs_mid, mod_x, w, fin, tm_x, final_norm=not update_ctx)

        if update_ctx:
            oa = _flash_call("A", qa_c, ka_c, va_c, None, None, tq=l_ctx, tk=tk, extras=a_extras,
                             lam_init=lam_init)
            ob = _flash_call("B", qb_c, kb_c, vb_c, None, None, tq=l_ctx, tk=tk)
            oc = _flash_call("C", qc_c, kc_c, vc_c, None, None, tq=l_ctx, tk=tk)
            od = _flash_call("C", qd_c, kd_c, vd_c, None, None, tq=l_ctx, tk=tk, sink=sink)
            xc_mid = _merge_call(xc, mod_c, (oa, ob, oc, od), w, l_ctx)
            xc = _ffn_call(xc_mid, mod_c, w, fin, l_ctx, final_norm=False)
        xs = xs_new
    return xs
```

```python
import functools
import math

import jax
import jax.numpy as jnp
from jax import lax
from jax.experimental import pallas as pl
from jax.experimental.pallas import tpu as pltpu

f32 = jnp.float32
bf16 = jnp.bfloat16

LANES = 128
HALF = 64
D_MODEL = 1024
GRID_W = 64
WINDOW = 128
N_BRANCH = 4
N_MOD = 6
EPS = 1e-6
ROPE_THETA = 10000.0
NEG_INF = -1e30
LOG2E = math.log2(math.e)
DIFF_HEADS = 4
DIFF_QK_DIM = 64
MLA_HEADS = 8
MLA_Q_LORA = 256
MLA_KV_LORA = 256
MLA_NOPE_DIM = 64
MLA_ROPE_DIM = 32
MLA_V_DIM = 64
GQA_HEADS = 8
BRANCH_WIDTH = 512
D_FF = 2816
FFN_CHUNKS = (1536, 1280)
assert sum(FFN_CHUNKS) == D_FF
HALO = 8
VMEM_LIMIT = 56 * 1024 * 1024

_OFF = {}
_o = 0
for _name, _w in (("aq", 512), ("ak", 512), ("av", 512), ("bq", 256), ("bkv", 288), ("cq", 512), ("ck", 128),
                  ("cv", 128), ("dq", 512), ("dk", 128), ("dv", 128), ("g", 4096)):
    _OFF[_name] = (_o, _o + _w)
    _o += _w
_PAIR_HEAD_ORDER = (0, 4, 1, 5, 2, 6, 3, 7)


def _dot(a, b):
    return jnp.dot(a, b, preferred_element_type=f32)


def _dot_nt(a, b):
    return lax.dot_general(a, b, (((1,), (1,)), ((), ())), preferred_element_type=f32)


def _sigmoid(v):
    return 1.0 / (1.0 + jnp.exp(-v))


def _rms(v):
    return v * lax.rsqrt(jnp.mean(v * v, axis=-1, keepdims=True) + EPS)


def _modulate(v, shift, scale):
    return _rms(v) * (1.0 + scale) + shift


def _const_spec(shape):
    zeros = (0,) * len(shape)
    return pl.BlockSpec(shape, lambda *_: zeros, pipeline_mode=pl.Buffered(1))


def _mod_kernel(c_ref, w_ref, b_ref, o_ref):
    cv = c_ref[...]
    a = cv * _sigmoid(cv)
    o_ref[...] = _dot(a, w_ref[...]) + b_ref[...]


def _mod_call(cs, w_mod, b_mod):
    depth, d, width = w_mod.shape
    tn = 1536
    return pl.pallas_call(
        _mod_kernel,
        name="mod_vectors",
        grid=(depth, width // tn),
        in_specs=[
            pl.BlockSpec((8, d), lambda l, j: (0, 0)),
            pl.BlockSpec((None, d, tn), lambda l, j: (l, 0, j)),
            pl.BlockSpec((None, 1, tn), lambda l, j: (l, 0, j)),
        ],
        out_specs=pl.BlockSpec((None, 8, tn), lambda l, j: (l, 0, j)),
        out_shape=jax.ShapeDtypeStruct((depth, 8, width), f32),
        compiler_params=pltpu.CompilerParams(vmem_limit_bytes=VMEM_LIMIT),
    )(cs, w_mod, b_mod.reshape(depth, 1, width))


def _proj_kernel(x_ref, mod_ref, c64_ref, s64_ref, cb_ref, sb_ref, w1_ref, qn_ref, kvn_ref, wuq_ref, wk_ref,
                 wv_ref, gq_ref, gk_ref,
                 qa_ref, ka_ref, va_ref, qb_ref, kb_ref, vb_ref, qc_ref, kc_ref, vc_ref, qd_ref, kd_ref, vd_ref,
                 *, scale_b):
    tm = x_ref.shape[0]
    h = _modulate(x_ref[...], mod_ref[0:1, :], mod_ref[1:2, :]).astype(bf16)
    lane = lax.broadcasted_iota(jnp.int32, (tm, LANES), 1)
    even = (lane & 1) == 0
    lo = lane < HALF
    c64, s64, cb, sb = c64_ref[...], s64_ref[...], cb_ref[...], sb_ref[...]

    def rope(v, cos, sin):
        partner = jnp.where(even, pltpu.roll(v, LANES - 1, 1), pltpu.roll(v, 1, 1))
        return v * cos + partner * sin

    def head_norm(v, gain):
        sq = v * v
        s_lo = jnp.sum(jnp.where(lo, sq, 0.0), axis=-1, keepdims=True)
        s_hi = jnp.sum(jnp.where(lo, 0.0, sq), axis=-1, keepdims=True)
        inv = jnp.where(lo, lax.rsqrt(s_lo * (1.0 / HALF) + EPS), lax.rsqrt(s_hi * (1.0 / HALF) + EPS))
        return v * inv * gain

    def chunk(z, j):
        return z[:, j * LANES:(j + 1) * LANES]

    qk_scale = DIFF_QK_DIM ** -0.5 * LOG2E

    z = _dot(h, w1_ref[:, 0:1536])
    for j in range(4):
        qa_ref[j] = (rope(chunk(z, j), c64, s64) * qk_scale).astype(bf16)
        ka_ref[j] = rope(chunk(z, 4 + j), c64, s64).astype(bf16)
        va_ref[j] = chunk(z, 8 + j).astype(bf16)

    z = _dot(h, w1_ref[:, 1536:2176])
    dq = (_rms(z[:, 0:256]) * qn_ref[...]).astype(bf16)
    ckv = (_rms(z[:, 256:512]) * kvn_ref[...]).astype(bf16)
    k_rot = rope(z[:, 512:640], cb, sb)
    zq = _dot(dq, wuq_ref[...])
    zk = _dot(ckv, wk_ref[...])
    zv = _dot(ckv, wv_ref[...])
    for j in range(MLA_HEADS):
        qb_ref[j] = (rope(chunk(zq, j), cb, sb) * scale_b).astype(bf16)
        kb_ref[j] = (chunk(zk, j) + k_rot).astype(bf16)
    for j in range(4):
        vb_ref[j] = chunk(zv, j).astype(bf16)

    z = _dot(h, w1_ref[:, 2176:2944])
    gq, gk = gq_ref[...], gk_ref[...]
    for j in range(4):
        qc_ref[j] = (rope(head_norm(chunk(z, j), gq), c64, s64) * qk_scale).astype(bf16)
    kc_ref[0] = rope(head_norm(chunk(z, 4), gk), c64, s64).astype(bf16)
    vc_ref[0] = chunk(z, 5).astype(bf16)

    z = _dot(h, w1_ref[:, 2944:3712])
    for j in range(4):
        qd_ref[j] = (rope(chunk(z, j), c64, s64) * qk_scale).astype(bf16)
    kd_ref[0] = rope(chunk(z, 4), c64, s64).astype(bf16)
    vd_ref[0] = chunk(z, 5).astype(bf16)


def _proj_call(xs, mod, tables, w, tm):
    b, t, d = xs.shape
    c64, s64, cb, sb = tables
    grid = (b, t // tm)

    def out(nchunks):
        return (jax.ShapeDtypeStruct((b, nchunks, t, LANES), bf16),
                pl.BlockSpec((None, nchunks, tm, LANES), lambda bi, i: (bi, 0, i, 0)))

    outs = [out(n) for n in (4, 4, 4, 8, 8, 4, 4, 1, 1, 4, 1, 1)]
    tab_spec = pl.BlockSpec((tm, LANES), lambda bi, i: (i, 0))
    scale_b = (MLA_NOPE_DIM + MLA_ROPE_DIM) ** -0.5 * LOG2E
    return pl.pallas_call(
        functools.partial(_proj_kernel, scale_b=scale_b),
        name="in_proj",
        grid=grid,
        in_specs=[
            pl.BlockSpec((None, tm, d), lambda bi, i: (bi, i, 0)),
            pl.BlockSpec((None, N_MOD, d), lambda bi, i: (bi, 0, 0)),
            tab_spec, tab_spec, tab_spec, tab_spec,
            _const_spec(w["w1"].shape), _const_spec((1, 256)), _const_spec((1, 256)),
            _const_spec(w["wuq"].shape), _const_spec(w["wk"].shape), _const_spec(w["wv"].shape),
            _const_spec((1, LANES)), _const_spec((1, LANES)),
        ],
        out_specs=[o[1] for o in outs],
        out_shape=[o[0] for o in outs],
        compiler_params=pltpu.CompilerParams(vmem_limit_bytes=VMEM_LIMIT),
    )(xs, mod, c64, s64, cb, sb, w["w1"], w["qn"], w["kvn"], w["wuq"], w["wk"], w["wv"], w["gq"], w["gk"])


def _flash_kernel(*refs, mode, has_x, has_sink, tk, lam_init):
    refs = list(refs)
    q_ref, kc_ref, vc_ref = refs[:3]
    pos = 3
    if has_x:
        k_ref, v_ref = refs[pos:pos + 2]
        pos += 2
    if mode == "A":
        dl_ref, subln_ref = refs[pos:pos + 2]
        pos += 2
    if has_sink:
        sink_ref = refs[pos]
        pos += 1
    o_ref, q2_ref, m_ref, acc_ref = refs[pos:pos + 4]
    s_refs, p_refs, al_refs = refs[pos + 4:pos + 6], refs[pos + 6:pos + 8], refs[pos + 8:pos + 10]

    tq = q_ref.shape[1]
    lane = lax.broadcasted_iota(jnp.int32, (tq, LANES), 1)
    lo = lane < HALF
    shared_k = mode != "B"
    if shared_k:
        q = q_ref[0]
        zero = jnp.zeros_like(q)
        q2_ref[0:tq, :] = jnp.where(lo, q, zero)
        q2_ref[tq:2 * tq, :] = jnp.where(lo, zero, q)
    else:
        q2_ref[0:tq, :] = q_ref[0]
        q2_ref[tq:2 * tq, :] = q_ref[1]

    for u in range(2):
        if has_sink:
            sink = sink_ref[pl.program_id(1) + 4 * u] * LOG2E
            m_ref[u] = jnp.full((tq, LANES), sink, f32)
            acc_ref[u, :, LANES:] = jnp.ones((tq, LANES), f32)
        else:
            m_ref[u] = jnp.full((tq, LANES), NEG_INF, f32)
            acc_ref[u, :, LANES:] = jnp.zeros((tq, LANES), f32)
        acc_ref[u, :, :LANES] = jnp.zeros((tq, LANES), f32)

    def scores(chunk, s_ref):
        kref, _, start, size = chunk
        if shared_k:
            s_ref[:, :size] = _dot_nt(q2_ref[...], kref[0, start:start + size, :])
        else:
            for u in range(2):
                rows = slice(u * tq, (u + 1) * tq)
                s_ref[rows, :size] = _dot_nt(q2_ref[rows, :], kref[u, start:start + size, :])

    def softmax(chunk, s_ref, p_ref, al_ref):
        size = chunk[3]
        for u in range(2):
            rows = slice(u * tq, (u + 1) * tq)
            cols = [s_ref[rows, j * LANES:(j + 1) * LANES] for j in range(size // LANES)]
            m_prev = m_ref[u]
            m_new = jnp.maximum(m_prev, jnp.max(functools.reduce(jnp.maximum, cols), axis=-1, keepdims=True))
            al_ref[u] = jnp.exp2(m_prev - m_new)
            for j, cj in enumerate(cols):
                p_ref[rows, j * LANES:(j + 1) * LANES] = jnp.exp2(cj - m_new).astype(bf16)
            m_ref[u] = m_new

    def pv(chunk, p_ref, al_ref):
        _, vref, start, size = chunk
        vext = jnp.concatenate([vref[0, start:start + size, :], jnp.ones((size, LANES), bf16)], axis=1)
        for u in range(2):
            alpha = al_ref[u]
            alpha2 = jnp.concatenate([alpha, alpha], axis=1)
            acc_ref[u] = alpha2 * acc_ref[u] + _dot(p_ref[u * tq:(u + 1) * tq, :size], vext)

    chunks = [(kc_ref, vc_ref, 0, kc_ref.shape[1])]
    if has_x:
        chunks += [(k_ref, v_ref, c * tk, tk) for c in range(k_ref.shape[1] // tk)]

    for i in range(len(chunks) + 2):
        if i < len(chunks):
            scores(chunks[i], s_refs[i % 2])
        if 1 <= i <= len(chunks):
            softmax(chunks[i - 1], s_refs[(i - 1) % 2], p_refs[(i - 1) % 2], al_refs[(i - 1) % 2])
        if i >= 2:
            pv(chunks[i - 2], p_refs[i % 2], al_refs[i % 2])

    o0 = acc_ref[0, :, :LANES] / acc_ref[0, :, LANES:]
    o1 = acc_ref[1, :, :LANES] / acc_ref[1, :, LANES:]
    if mode == "A":
        dl = dl_ref[...]
        lam = (jnp.exp(jnp.sum(dl[0:1] * dl[1:2], axis=-1, keepdims=True))
               - jnp.exp(jnp.sum(dl[2:3] * dl[3:4], axis=-1, keepdims=True)) + lam_init)
        o = _rms(o0 - lam * o1) * subln_ref[...] * (1.0 - lam_init)
    else:
        o = jnp.where(lo, o0, o1)
    o_ref[...] = o.astype(bf16)


def _flash_call(mode, q, kc, vc, k, v, *, tq, tk, extras=(), sink=None, lam_init=0.0):
    b, _, t, _ = q.shape
    l_ctx = kc.shape[2]
    has_x = k is not None
    gq = 2 if mode == "B" else 1
    shared_kv = mode == "C"

    def kv_idx(bi, g, i):
        return (bi, 0 if shared_kv else g, 0, 0)

    in_specs = [
        pl.BlockSpec((None, gq, tq, LANES), lambda bi, g, i: (bi, g, i, 0)),
        pl.BlockSpec((None, gq, l_ctx, LANES), kv_idx),
        pl.BlockSpec((None, 1, l_ctx, LANES), kv_idx),
    ]
    args = [q, kc, vc]
    if has_x:
        n = k.shape[2]
        in_specs += [pl.BlockSpec((None, gq, n, LANES), kv_idx), pl.BlockSpec((None, 1, n, LANES), kv_idx)]
        args += [k, v]
    for e in extras:
        in_specs.append(pl.BlockSpec(e.shape, lambda bi, g, i, nd=e.ndim: (0,) * nd))
        args.append(e)
    if sink is not None:
        in_specs.append(pl.BlockSpec(memory_space=pltpu.SMEM))
        args.append(sink)
    state = pltpu.VMEM((2, tq, LANES), f32)
    width = max(tk, l_ctx) if has_x else l_ctx
    if has_x:
        assert k.shape[2] % tk == 0 and tk % LANES == 0
    s_buf, p_buf = pltpu.VMEM((2 * tq, width), f32), pltpu.VMEM((2 * tq, width), bf16)
    scratch = [pltpu.VMEM((2 * tq, LANES), bf16), state, pltpu.VMEM((2, tq, 2 * LANES), f32),
               s_buf, s_buf, p_buf, p_buf, state, state]
    return pl.pallas_call(
        functools.partial(_flash_kernel, mode=mode, has_x=has_x, has_sink=sink is not None, tk=tk,
                          lam_init=lam_init),
        name="flash_" + mode + ("" if has_x else "_ctx"),
        grid=(b, 4, t // tq),
        in_specs=in_specs,
        out_specs=pl.BlockSpec((None, tq, LANES), lambda bi, g, i: (bi, i, g)),
        out_shape=jax.ShapeDtypeStruct((b, t, 4 * LANES), bf16),
        scratch_shapes=scratch,
        compiler_params=pltpu.CompilerParams(vmem_limit_bytes=VMEM_LIMIT),
    )(*args)


def _window_kernel(q_ref, kc_ref, vc_ref, k_ref, v_ref, sink_ref, o_ref):
    tq = q_ref.shape[1]
    n = k_ref.shape[1]
    l_ctx = kc_ref.shape[1]
    band = tq + 2 * WINDOW
    width = l_ctx + band
    t0 = pl.program_id(1) * tq
    start = pl.multiple_of(jnp.clip(t0 - WINDOW, 0, n - band), WINDOW)
    k_all = jnp.concatenate([kc_ref[0], k_ref[0, pl.ds(start, band), :]], axis=0)
    v_all = jnp.concatenate([vc_ref[0], v_ref[0, pl.ds(start, band), :]], axis=0)
    v_ext = jnp.concatenate([v_all, jnp.ones_like(v_all)], axis=1)
    col = lax.broadcasted_iota(jnp.int32, (tq, width), 1)
    qpos = t0 + lax.broadcasted_iota(jnp.int32, (tq, width), 0)
    valid = (col < l_ctx) | (jnp.abs(col - l_ctx + start - qpos) <= WINDOW)
    lane = lax.broadcasted_iota(jnp.int32, (tq, LANES), 1)
    lo = lane < HALF
    for j in range(4):
        q = q_ref[j]
        zero = jnp.zeros_like(q)
        s = _dot_nt(jnp.concatenate([jnp.where(lo, q, zero), jnp.where(lo, zero, q)], axis=0), k_all)
        outs = []
        for u in range(2):
            sink = sink_ref[j + 4 * u] * LOG2E
            su = jnp.where(valid, s[u * tq:(u + 1) * tq], NEG_INF)
            cols = [su[:, c * LANES:(c + 1) * LANES] for c in range(width // LANES)]
            m = jnp.maximum(jnp.max(functools.reduce(jnp.maximum, cols), axis=-1, keepdims=True), sink)
            p = jnp.concatenate([jnp.exp2(c - m).astype(bf16) for c in cols], axis=1)
            acc = _dot(p, v_ext)
            outs.append(acc[:, :LANES] / (acc[:, LANES:] + jnp.exp2(sink - m)))
        o_ref[:, j * LANES:(j + 1) * LANES] = jnp.where(lo, outs[0], outs[1]).astype(bf16)


def _window_call(q, kc, vc, k, v, sink, *, tq):
    b, _, n, _ = q.shape
    l_ctx = kc.shape[2]
    return pl.pallas_call(
        _window_kernel,
        name="window_attn",
        grid=(b, n // tq),
        in_specs=[
            pl.BlockSpec((None, 4, tq, LANES), lambda bi, i: (bi, 0, i, 0)),
            pl.BlockSpec((None, 1, l_ctx, LANES), lambda bi, i: (bi, 0, 0, 0)),
            pl.BlockSpec((None, 1, l_ctx, LANES), lambda bi, i: (bi, 0, 0, 0)),
            pl.BlockSpec((None, 1, n, LANES), lambda bi, i: (bi, 0, 0, 0)),
            pl.BlockSpec((None, 1, n, LANES), lambda bi, i: (bi, 0, 0, 0)),
            pl.BlockSpec(memory_space=pltpu.SMEM),
        ],
        out_specs=pl.BlockSpec((None, tq, 4 * LANES), lambda bi, i: (bi, i, 0)),
        out_shape=jax.ShapeDtypeStruct((b, n, 4 * LANES), bf16),
        compiler_params=pltpu.CompilerParams(vmem_limit_bytes=VMEM_LIMIT),
    )(q, kc, vc, k, v, sink)


def _merge_kernel(x_ref, mod_ref, oa_ref, ob_ref, oc_ref, od_ref, wg_ref, wbr_ref, wout_ref, o_ref):
    x = x_ref[...]
    h = _modulate(x, mod_ref[0:1, :], mod_ref[1:2, :]).astype(bf16)
    mix = None
    for k, br_ref in enumerate((oa_ref, ob_ref, oc_ref, od_ref)):
        g = _dot(h, wg_ref[:, k * D_MODEL:(k + 1) * D_MODEL])
        y = _dot(br_ref[...], wbr_ref[k])
        term = _sigmoid(g) * y
        mix = term if mix is None else mix + term
    o_ref[...] = x + mod_ref[2:3, :] * _dot(mix.astype(bf16), wout_ref[...])


def _merge_call(xs, mod, branches, w, tm):
    b, t, d = xs.shape
    tiles = t // tm
    x2 = xs.reshape(b * t, d)
    br = [o.reshape(b * t, BRANCH_WIDTH) for o in branches]
    br_spec = pl.BlockSpec((tm, BRANCH_WIDTH), lambda i: (i, 0))
    out = pl.pallas_call(
        _merge_kernel,
        name="merge",
        grid=(b * tiles,),
        in_specs=[
            pl.BlockSpec((tm, d), lambda i: (i, 0)),
            pl.BlockSpec((None, N_MOD, d), lambda i: (i // tiles, 0, 0)),
            br_spec, br_spec, br_spec, br_spec,
            _const_spec(w["wg"].shape), _const_spec(w["wbr"].shape), _const_spec(w["wout"].shape),
        ],
        out_specs=pl.BlockSpec((tm, d), lambda i: (i, 0)),
        out_shape=jax.ShapeDtypeStruct((b * t, d), f32),
        compiler_params=pltpu.CompilerParams(vmem_limit_bytes=VMEM_LIMIT),
    )(x2, mod, *br, w["wg"], w["wbr"], w["wout"])
    return out.reshape(b, t, d)


def _ffn_kernel(x_ref, xp_ref, xn_ref, mod_ref, wup_ref, cw_ref, cb_ref, wdn_ref, fin_ref, o_ref, u_ref, *,
                seq_len, final_norm):
    tm = x_ref.shape[0]
    x = x_ref[...]
    shift, scale = mod_ref[3:4, :], mod_ref[4:5, :]
    tile = pl.program_id(0) % (seq_len // tm)
    keep_prev = (tile != 0).astype(f32)
    keep_next = (tile != seq_len // tm - 1).astype(f32)
    h = jnp.concatenate([
        (_modulate(xp_ref[...], shift, scale) * keep_prev).astype(bf16),
        _modulate(x, shift, scale).astype(bf16),
        (_modulate(xn_ref[...], shift, scale) * keep_next).astype(bf16)], axis=0)

    def conv(c0, width):
        cols = slice(c0, c0 + width)
        u_ref[:, cols] = _dot(h, wup_ref[:, cols])
        prev = u_ref[HALO - 1:HALO - 1 + tm, cols]
        cur = u_ref[HALO:HALO + tm, cols]
        nxt = u_ref[HALO + 1:HALO + 1 + tm, cols]
        return prev * cw_ref[0:1, cols] + cur * cw_ref[1:2, cols] + nxt * cw_ref[2:3, cols] + cb_ref[:, cols]

    acc = None
    cv = 0
    for width in FFN_CHUNKS:
        val = conv(cv, width)
        gate = conv(D_FF + cv, width)
        a = (gate * _sigmoid(gate) * val).astype(bf16)
        part = _dot(a, wdn_ref[cv:cv + width, :])
        acc = part if acc is None else acc + part
        cv += width
    out = x + mod_ref[5:6, :] * acc
    if final_norm:
        out = _rms(out) * fin_ref[...]
    o_ref[...] = out


def _ffn_call(xs, mod, w, fin, tm, final_norm):
    b, t, d = xs.shape
    tiles = t // tm
    x2 = xs.reshape(b * t, d)
    hb = tm // HALO
    last_halo = b * t // HALO - 1
    out = pl.pallas_call(
        functools.partial(_ffn_kernel, seq_len=t, final_norm=final_norm),
        name="conv_ffn",
        grid=(b * tiles,),
        in_specs=[
            pl.BlockSpec((tm, d), lambda i: (i, 0)),
            pl.BlockSpec((HALO, d), lambda i: (jnp.maximum(i * hb - 1, 0), 0)),
            pl.BlockSpec((HALO, d), lambda i: (jnp.minimum((i + 1) * hb, last_halo), 0)),
            pl.BlockSpec((None, N_MOD, d), lambda i: (i // tiles, 0, 0)),
            _const_spec(w["wup"].shape), _const_spec(w["cw"].shape), _const_spec(w["cb"].shape),
            _const_spec(w["wdn"].shape), _const_spec((1, d)),
        ],
        out_specs=pl.BlockSpec((tm, d), lambda i: (i, 0)),
        out_shape=jax.ShapeDtypeStruct((b * t, d), f32),
        scratch_shapes=[pltpu.VMEM((tm + 2 * HALO, 2 * D_FF), f32)],
        compiler_params=pltpu.CompilerParams(vmem_limit_bytes=VMEM_LIMIT),
    )(x2, x2, x2, mod, w["wup"], w["cw"], w["cb"], w["wdn"], fin)
    return out.reshape(b, t, d)


def _cols(w, name):
    a, b = _OFF[name]
    return w[:, a:b]


def _pair_heads(w_cols):
    d = w_cols.shape[0]
    return w_cols.reshape(d, GQA_HEADS, HALF)[:, jnp.array(_PAIR_HEAD_ORDER), :].reshape(d, GQA_HEADS * HALF)


def _layer_weights(l, w_in, mla_q_norm, mla_kv_norm, mla_w_uq, mla_w_ukv, gqa_q_norm, gqa_k_norm, w_branch,
                   w_out, ffn_w_up, ffn_conv_w, ffn_conv_b, ffn_w_down):
    wi = w_in[l]
    d = wi.shape[0]
    bkv = _cols(wi, "bkv")
    k_rot = jnp.concatenate([jnp.zeros((d, MLA_NOPE_DIM), wi.dtype), bkv[:, MLA_KV_LORA:],
                             jnp.zeros((d, LANES - MLA_NOPE_DIM - MLA_ROPE_DIM), wi.dtype)], axis=1)
    w1 = jnp.concatenate([
        _cols(wi, "aq"), _cols(wi, "ak"), _cols(wi, "av"),
        _cols(wi, "bq"), bkv[:, :MLA_KV_LORA], k_rot,
        _pair_heads(_cols(wi, "cq")), _cols(wi, "ck"), _cols(wi, "cv"),
        _pair_heads(_cols(wi, "dq")), _cols(wi, "dk"), _cols(wi, "dv")], axis=1).astype(bf16)
    hq = MLA_NOPE_DIM + MLA_ROPE_DIM
    wuq = mla_w_uq[l].reshape(MLA_Q_LORA, MLA_HEADS, hq)
    wuq = jnp.pad(wuq, ((0, 0), (0, 0), (0, LANES - hq))).reshape(MLA_Q_LORA, MLA_HEADS * LANES).astype(bf16)
    wukv = mla_w_ukv[l].reshape(MLA_KV_LORA, MLA_HEADS, MLA_NOPE_DIM + MLA_V_DIM)
    wk = jnp.pad(wukv[:, :, :MLA_NOPE_DIM], ((0, 0), (0, 0), (0, LANES - MLA_NOPE_DIM)))
    wk = wk.reshape(MLA_KV_LORA, MLA_HEADS * LANES).astype(bf16)
    wv = wukv[:, :, MLA_NOPE_DIM:].reshape(MLA_KV_LORA, MLA_HEADS * MLA_V_DIM).astype(bf16)
    wb = w_branch[l]
    order = jnp.array(_PAIR_HEAD_ORDER)
    paired_rows = lambda m: m.reshape(GQA_HEADS, HALF, -1)[order].reshape(BRANCH_WIDTH, -1)
    wbr = jnp.stack([wb[0], wb[1], paired_rows(wb[2]), paired_rows(wb[3])]).astype(bf16)
    return dict(
        w1=w1, wuq=wuq, wk=wk, wv=wv,
        qn=mla_q_norm[l].reshape(1, -1), kvn=mla_kv_norm[l].reshape(1, -1),
        gq=jnp.tile(gqa_q_norm[l], 2).reshape(1, LANES), gk=jnp.tile(gqa_k_norm[l], 2).reshape(1, LANES),
        wg=_cols(wi, "g").astype(bf16), wbr=wbr, wout=w_out[l].astype(bf16),
        wup=ffn_w_up[l].astype(bf16), cw=ffn_conv_w[l], cb=ffn_conv_b[l].reshape(1, -1),
        wdn=ffn_w_down[l].astype(bf16))


def _rope_tables(n, l_ctx):
    t = jnp.arange(n, dtype=jnp.int32)
    row, col = (t // GRID_W).astype(f32), (t % GRID_W).astype(f32)

    def expanded(rot_dim):
        axis_dim = rot_dim // 2
        inv = ROPE_THETA ** (-jnp.arange(0, axis_dim, 2, dtype=f32) / axis_dim)
        ang = jnp.concatenate([row[:, None] * inv, col[:, None] * inv], axis=-1)
        cos = jnp.repeat(jnp.cos(ang), 2, axis=-1)
        sin = jnp.repeat(jnp.sin(ang), 2, axis=-1) * jnp.tile(jnp.array([-1.0, 1.0], f32), rot_dim // 2)
        return cos, sin

    c64, s64 = expanded(2 * HALF // 2)
    c64, s64 = jnp.tile(c64, (1, 2)), jnp.tile(s64, (1, 2))
    c32, s32 = expanded(MLA_ROPE_DIM)
    pad = LANES - MLA_NOPE_DIM - MLA_ROPE_DIM
    cb = jnp.concatenate([jnp.ones((n, MLA_NOPE_DIM), f32), c32, jnp.ones((n, pad), f32)], axis=1)
    sb = jnp.concatenate([jnp.zeros((n, MLA_NOPE_DIM), f32), s32, jnp.zeros((n, pad), f32)], axis=1)
    ones, zeros = jnp.ones((l_ctx, LANES), f32), jnp.zeros((l_ctx, LANES), f32)
    return (c64, s64, cb, sb), (ones, zeros, ones, zeros)


def kernel(x, c, ctx, c_ctx, w_mod, b_mod, w_in, diff_lambda, diff_subln, mla_q_norm, mla_kv_norm, mla_w_uq,
           mla_w_ukv, gqa_q_norm, gqa_k_norm, swa_sink, w_branch, w_out, ffn_w_up, ffn_conv_w, ffn_conv_b,
           ffn_w_down, final_norm):
    b, n, d = x.shape
    l_ctx = ctx.shape[1]
    depth = w_mod.shape[0]
    tm_x = min(512, n)
    tq = min(256, n)
    tk = min(1024, n)

    cs = jnp.zeros((8, d), f32).at[:b].set(c).at[b].set(c_ctx)
    mods = _mod_call(cs, w_mod, b_mod).reshape(depth, 8, N_MOD, d)
    tab_x, tab_c = _rope_tables(n, l_ctx)
    fin = final_norm.reshape(1, d)

    w_in, mla_w_uq, mla_w_ukv, w_branch, w_out, ffn_w_up, ffn_w_down = (
        a.astype(bf16) for a in (w_in, mla_w_uq, mla_w_ukv, w_branch, w_out, ffn_w_up, ffn_w_down))

    xs, xc = x, ctx
    for l in range(depth):
        w = _layer_weights(l, w_in, mla_q_norm, mla_kv_norm, mla_w_uq, mla_w_ukv, gqa_q_norm, gqa_k_norm,
                           w_branch, w_out, ffn_w_up, ffn_conv_w, ffn_conv_b, ffn_w_down)
        mod_x = mods[l, :b]
        mod_c = jnp.broadcast_to(mods[l, b][None], (b, N_MOD, d))
        lam_init = 0.8 - 0.6 * math.exp(-0.3 * l)
        a_extras = (diff_lambda[l], diff_subln[l].reshape(1, LANES))
        sink = swa_sink[l]
        update_ctx = l < depth - 1

        px = _proj_call(xs, mod_x, tab_x, w, tm_x)
        pc = _proj_call(xc, mod_c, tab_c, w, l_ctx)
        qa, ka, va, qb, kb, vb, qc, kc, vc, qd, kd, vd = px
        qa_c, ka_c, va_c, qb_c, kb_c, vb_c, qc_c, kc_c, vc_c, qd_c, kd_c, vd_c = pc

        oa = _flash_call("A", qa, ka_c, va_c, ka, va, tq=tq, tk=tk, extras=a_extras, lam_init=lam_init)
        ob = _flash_call("B", qb, kb_c, vb_c, kb, vb, tq=tq, tk=tk)
        oc = _flash_call("C", qc, kc_c, vc_c, kc, vc, tq=tq, tk=tk)
        od = _window_call(qd, kd_c, vd_c, kd, vd, sink, tq=min(256, n))
        xs_mid = _merge_call(xs, mod_x, (oa, ob, oc, od), w, tm_x)
        xs_new = _ffn_call(xs_mid, mod_x, w, fin, tm_x, final_norm=not update_ctx)

        if update_ctx:
            oa = _flash_call("A", qa_c, ka_c, va_c, None, None, tq=l_ctx, tk=tk, extras=a_extras,
                             lam_init=lam_init)
            ob = _flash_call("B", qb_c, kb_c, vb_c, None, None, tq=l_ctx, tk=tk)
            oc = _flash_call("C", qc_c, kc_c, vc_c, None, None, tq=l_ctx, tk=tk)
            od = _flash_call("C", qd_c, kd_c, vd_c, None, None, tq=l_ctx, tk=tk, sink=sink)
            xc_mid = _merge_call(xc, mod_c, (oa, ob, oc, od), w, l_ctx)
            xc = _ffn_call(xc_mid, mod_c, w, fin, l_ctx, final_norm=False)
        xs = xs_new
    return xs
```

```python
import functools
import math

import jax
import jax.numpy as jnp
from jax import lax
from jax.experimental import pallas as pl
from jax.experimental.pallas import tpu as pltpu

f32 = jnp.float32
bf16 = jnp.bfloat16

LANES = 128
HALF = 64
D_MODEL = 1024
GRID_W = 64
WINDOW = 128
N_BRANCH = 4
N_MOD = 6
EPS = 1e-6
ROPE_THETA = 10000.0
NEG_INF = -1e30
LOG2E = math.log2(math.e)
DIFF_HEADS = 4
DIFF_QK_DIM = 64
MLA_HEADS = 8
MLA_Q_LORA = 256
MLA_KV_LORA = 256
MLA_NOPE_DIM = 64
MLA_ROPE_DIM = 32
MLA_V_DIM = 64
GQA_HEADS = 8
BRANCH_WIDTH = 512
D_FF = 2816
FFN_CHUNKS = (1536, 1280)
assert sum(FFN_CHUNKS) == D_FF
HALO = 8
VMEM_LIMIT = 56 * 1024 * 1024

IN_WIDTH = 7712
ALIGNED_COLS = 2048
ROT_COLS = 32
W1_WIDTH = 4224
GATE_WIDTH = N_BRANCH * D_MODEL


def _dot(a, b):
    return jnp.dot(a, b, preferred_element_type=f32)


def _dot_nt(a, b):
    return lax.dot_general(a, b, (((1,), (1,)), ((), ())), preferred_element_type=f32)


def _sigmoid(v):
    return 1.0 / (1.0 + jnp.exp(-v))


def _rms(v):
    return v * lax.rsqrt(jnp.mean(v * v, axis=-1, keepdims=True) + EPS)


def _modulate(v, shift, scale):
    return _rms(v) * (1.0 + scale) + shift


def _const_spec(shape):
    zeros = (0,) * len(shape)
    return pl.BlockSpec(shape, lambda *_: zeros, pipeline_mode=pl.Buffered(1))


def _layer_spec(stacked, l):
    tail = stacked.shape[1:]
    idx = (l,) + (0,) * len(tail)
    return pl.BlockSpec((None,) + tail, lambda *_: idx, pipeline_mode=pl.Buffered(1))


def _mod_kernel(c_ref, w_ref, b_ref, o_ref):
    cv = c_ref[...]
    a = cv * _sigmoid(cv)
    o_ref[...] = _dot(a, w_ref[...]) + b_ref[...]


def _mod_call(cs, w_mod, b_mod):
    depth, d, width = w_mod.shape
    tn = 1536
    return pl.pallas_call(
        _mod_kernel,
        name="mod_vectors",
        grid=(depth, width // tn),
        in_specs=[
            pl.BlockSpec((8, d), lambda l, j: (0, 0)),
            pl.BlockSpec((None, d, tn), lambda l, j: (l, 0, j)),
            pl.BlockSpec((None, 1, tn), lambda l, j: (l, 0, j)),
        ],
        out_specs=pl.BlockSpec((None, 8, tn), lambda l, j: (l, 0, j)),
        out_shape=jax.ShapeDtypeStruct((depth, 8, width), f32),
        compiler_params=pltpu.CompilerParams(vmem_limit_bytes=VMEM_LIMIT),
    )(cs, w_mod, b_mod.reshape(depth, 1, width))


def _repack_kernel(w_ref, w1_ref, wg_ref):
    tr = w_ref.shape[0]
    lane = lax.broadcasted_iota(jnp.int32, (tr, LANES), 1)

    def src(i):
        return w_ref[:, i * LANES:(i + 1) * LANES]

    w1_ref[:, 0:ALIGNED_COLS] = w_ref[:, 0:ALIGNED_COLS].astype(bf16)
    first = ALIGNED_COLS // LANES
    rot = pltpu.roll(jnp.where(lane < ROT_COLS, src(first), 0.0), HALF, 1)
    w1_ref[:, ALIGNED_COLS:ALIGNED_COLS + LANES] = rot.astype(bf16)

    n_tail = (IN_WIDTH - ALIGNED_COLS - ROT_COLS) // LANES
    rolled = [pltpu.roll(src(first + c), LANES - ROT_COLS, 1) for c in range(n_tail + 1)]

    def tail(c):
        return jnp.where(lane < LANES - ROT_COLS, rolled[c], rolled[c + 1])

    def both_halves(v):
        r = pltpu.roll(v, HALF, 1)
        return jnp.where(lane < HALF, v, r), jnp.where(lane < HALF, r, v)

    out = ALIGNED_COLS + LANES
    c = 0
    for _ in range(2):
        for _ in range(4):
            w1_ref[:, out:out + LANES] = tail(c).astype(bf16)
            out, c = out + LANES, c + 1
        for _ in range(2):
            for piece in both_halves(tail(c)):
                w1_ref[:, out:out + LANES] = piece.astype(bf16)
                out += LANES
            c += 1
    assert out == W1_WIDTH and n_tail - c == GATE_WIDTH // LANES
    for j in range(GATE_WIDTH // LANES):
        wg_ref[:, j * LANES:(j + 1) * LANES] = tail(c + j).astype(bf16)


def _repack_call(w_in):
    depth, d, width = w_in.shape
    assert width == IN_WIDTH
    tr = 256
    padded = pl.cdiv(width, LANES) * LANES
    return pl.pallas_call(
        _repack_kernel,
        name="repack_w_in",
        grid=(depth, d // tr),
        in_specs=[pl.BlockSpec((None, tr, padded), lambda l, i: (l, i, 0))],
        out_specs=[pl.BlockSpec((None, tr, W1_WIDTH), lambda l, i: (l, i, 0)),
                   pl.BlockSpec((None, tr, GATE_WIDTH), lambda l, i: (l, i, 0))],
        out_shape=[jax.ShapeDtypeStruct((depth, d, W1_WIDTH), bf16),
                   jax.ShapeDtypeStruct((depth, d, GATE_WIDTH), bf16)],
        compiler_params=pltpu.CompilerParams(vmem_limit_bytes=VMEM_LIMIT),
    )(w_in)


def _proj_kernel(x_ref, mod_ref, c64_ref, s64_ref, cb_ref, sb_ref, w1_ref, qn_ref, kvn_ref, wuq_ref, wk_ref,
                 wv_ref, gq_ref, gk_ref,
                 qa_ref, ka_ref, va_ref, qb_ref, kb_ref, vb_ref, qc_ref, kc_ref, vc_ref, qd_ref, kd_ref, vd_ref,
                 *, scale_b):
    tm = x_ref.shape[0]
    h = _modulate(x_ref[...], mod_ref[0:1, :], mod_ref[1:2, :]).astype(bf16)
    lane = lax.broadcasted_iota(jnp.int32, (tm, LANES), 1)
    even = (lane & 1) == 0
    lo = lane < HALF
    c64, s64, cb, sb = c64_ref[...], s64_ref[...], cb_ref[...], sb_ref[...]

    def rope(v, cos, sin):
        partner = jnp.where(even, pltpu.roll(v, LANES - 1, 1), pltpu.roll(v, 1, 1))
        return v * cos + partner * sin

    def head_norm(v, gain):
        sq = v * v
        s_lo = jnp.sum(jnp.where(lo, sq, 0.0), axis=-1, keepdims=True)
        s_hi = jnp.sum(jnp.where(lo, 0.0, sq), axis=-1, keepdims=True)
        inv = jnp.where(lo, lax.rsqrt(s_lo * (1.0 / HALF) + EPS), lax.rsqrt(s_hi * (1.0 / HALF) + EPS))
        return v * inv * gain

    def chunk(z, j):
        return z[:, j * LANES:(j + 1) * LANES]

    qk_scale = DIFF_QK_DIM ** -0.5 * LOG2E

    z = _dot(h, w1_ref[:, 0:1536])
    for j in range(4):
        qa_ref[j] = (rope(chunk(z, j), c64, s64) * qk_scale).astype(bf16)
        ka_ref[j] = rope(chunk(z, 4 + j), c64, s64).astype(bf16)
        va_ref[j] = chunk(z, 8 + j).astype(bf16)

    z = _dot(h, w1_ref[:, 1536:2176])
    dq = (_rms(z[:, 0:256]) * qn_ref[...]).astype(bf16)
    ckv = (_rms(z[:, 256:512]) * kvn_ref[...]).astype(bf16)
    k_rot = rope(z[:, 512:640], cb, sb)
    zq = _dot(dq, wuq_ref[...])
    zk = _dot(ckv, wk_ref[...])
    zv = _dot(ckv, wv_ref[...])
    for j in range(MLA_HEADS):
        qb_ref[j] = (rope(chunk(zq, j), cb, sb) * scale_b).astype(bf16)
        kb_ref[j] = (chunk(zk, j) + k_rot).astype(bf16)
    for j in range(4):
        vb_ref[j] = chunk(zv, j).astype(bf16)

    z = _dot(h, w1_ref[:, 2176:3200])
    gq, gk = gq_ref[...], gk_ref[...]
    for j in range(4):
        qc_ref[j] = (rope(head_norm(chunk(z, j), gq), c64, s64) * qk_scale).astype(bf16)
    for j in range(2):
        kc_ref[j] = rope(head_norm(chunk(z, 4 + j), gk), c64, s64).astype(bf16)
        vc_ref[j] = chunk(z, 6 + j).astype(bf16)

    z = _dot(h, w1_ref[:, 3200:4224])
    for j in range(4):
        qd_ref[j] = (rope(chunk(z, j), c64, s64) * qk_scale).astype(bf16)
    for j in range(2):
        kd_ref[j] = rope(chunk(z, 4 + j), c64, s64).astype(bf16)
        vd_ref[j] = chunk(z, 6 + j).astype(bf16)


def _proj_call(xs, mod, tables, w, tm):
    b, t, d = xs.shape
    c64, s64, cb, sb = tables
    grid = (b, t // tm)

    def out(nchunks):
        return (jax.ShapeDtypeStruct((b, nchunks, t, LANES), bf16),
                pl.BlockSpec((None, nchunks, tm, LANES), lambda bi, i: (bi, 0, i, 0)))

    outs = [out(n) for n in (4, 4, 4, 8, 8, 4, 4, 2, 2, 4, 2, 2)]
    tab_spec = pl.BlockSpec((tm, LANES), lambda bi, i: (i, 0))
    scale_b = (MLA_NOPE_DIM + MLA_ROPE_DIM) ** -0.5 * LOG2E
    return pl.pallas_call(
        functools.partial(_proj_kernel, scale_b=scale_b),
        name="in_proj",
        grid=grid,
        in_specs=[
            pl.BlockSpec((None, tm, d), lambda bi, i: (bi, i, 0)),
            pl.BlockSpec((None, N_MOD, d), lambda bi, i: (bi, 0, 0)),
            tab_spec, tab_spec, tab_spec, tab_spec,
            _layer_spec(w["w1"], w["l"]), _const_spec((1, 256)), _const_spec((1, 256)),
            _const_spec(w["wuq"].shape), _const_spec(w["wk"].shape), _const_spec(w["wv"].shape),
            _const_spec((1, LANES)), _const_spec((1, LANES)),
        ],
        out_specs=[o[1] for o in outs],
        out_shape=[o[0] for o in outs],
        compiler_params=pltpu.CompilerParams(vmem_limit_bytes=VMEM_LIMIT),
    )(xs, mod, c64, s64, cb, sb, w["w1"], w["qn"], w["kvn"], w["wuq"], w["wk"], w["wv"], w["gq"], w["gk"])


def _flash_kernel(*refs, mode, has_x, has_sink, tk, lam_init):
    refs = list(refs)
    q_ref, kc_ref, vc_ref = refs[:3]
    pos = 3
    if has_x:
        k_ref, v_ref = refs[pos:pos + 2]
        pos += 2
    if mode == "A":
        dl_ref, subln_ref = refs[pos:pos + 2]
        pos += 2
    if has_sink:
        sink_ref = refs[pos]
        pos += 1
    o_ref, q2_ref, m_ref, acc_ref = refs[pos:pos + 4]
    s_refs, p_refs, al_refs = refs[pos + 4:pos + 6], refs[pos + 6:pos + 8], refs[pos + 8:pos + 10]

    tq = q_ref.shape[1]
    lane = lax.broadcasted_iota(jnp.int32, (tq, LANES), 1)
    lo = lane < HALF
    shared_k = mode != "B"
    if shared_k:
        q = q_ref[0]
        zero = jnp.zeros_like(q)
        q2_ref[0:tq, :] = jnp.where(lo, q, zero)
        q2_ref[tq:2 * tq, :] = jnp.where(lo, zero, q)
    else:
        q2_ref[0:tq, :] = q_ref[0]
        q2_ref[tq:2 * tq, :] = q_ref[1]

    for u in range(2):
        if has_sink:
            sink = sink_ref[2 * pl.program_id(1) + u] * LOG2E
            m_ref[u] = jnp.full((tq, LANES), sink, f32)
            acc_ref[u, :, LANES:] = jnp.ones((tq, LANES), f32)
        else:
            m_ref[u] = jnp.full((tq, LANES), NEG_INF, f32)
            acc_ref[u, :, LANES:] = jnp.zeros((tq, LANES), f32)
        acc_ref[u, :, :LANES] = jnp.zeros((tq, LANES), f32)

    def scores(chunk, s_ref):
        kref, _, start, size = chunk
        if shared_k:
            s_ref[:, :size] = _dot_nt(q2_ref[...], kref[0, start:start + size, :])
        else:
            for u in range(2):
                rows = slice(u * tq, (u + 1) * tq)
                s_ref[rows, :size] = _dot_nt(q2_ref[rows, :], kref[u, start:start + size, :])

    def softmax(chunk, s_ref, p_ref, al_ref):
        size = chunk[3]
        for u in range(2):
            rows = slice(u * tq, (u + 1) * tq)
            cols = [s_ref[rows, j * LANES:(j + 1) * LANES] for j in range(size // LANES)]
            m_prev = m_ref[u]
            m_new = jnp.maximum(m_prev, jnp.max(functools.reduce(jnp.maximum, cols), axis=-1, keepdims=True))
            al_ref[u] = jnp.exp2(m_prev - m_new)
            for j, cj in enumerate(cols):
                p_ref[rows, j * LANES:(j + 1) * LANES] = jnp.exp2(cj - m_new).astype(bf16)
            m_ref[u] = m_new

    def pv(chunk, p_ref, al_ref):
        _, vref, start, size = chunk
        vext = jnp.concatenate([vref[0, start:start + size, :], jnp.ones((size, LANES), bf16)], axis=1)
        for u in range(2):
            alpha = al_ref[u]
            alpha2 = jnp.concatenate([alpha, alpha], axis=1)
            acc_ref[u] = alpha2 * acc_ref[u] + _dot(p_ref[u * tq:(u + 1) * tq, :size], vext)

    chunks = [(kc_ref, vc_ref, 0, kc_ref.shape[1])]
    if has_x:
        chunks += [(k_ref, v_ref, c * tk, tk) for c in range(k_ref.shape[1] // tk)]

    for i in range(len(chunks) + 2):
        if i < len(chunks):
            scores(chunks[i], s_refs[i % 2])
        if 1 <= i <= len(chunks):
            softmax(chunks[i - 1], s_refs[(i - 1) % 2], p_refs[(i - 1) % 2], al_refs[(i - 1) % 2])
        if i >= 2:
            pv(chunks[i - 2], p_refs[i % 2], al_refs[i % 2])

    o0 = acc_ref[0, :, :LANES] / acc_ref[0, :, LANES:]
    o1 = acc_ref[1, :, :LANES] / acc_ref[1, :, LANES:]
    if mode == "A":
        dl = dl_ref[...]
        lam = (jnp.exp(jnp.sum(dl[0:1] * dl[1:2], axis=-1, keepdims=True))
               - jnp.exp(jnp.sum(dl[2:3] * dl[3:4], axis=-1, keepdims=True)) + lam_init)
        o = _rms(o0 - lam * o1) * subln_ref[...] * (1.0 - lam_init)
    else:
        o = jnp.where(lo, o0, o1)
    o_ref[...] = o.astype(bf16)


def _flash_call(mode, q, kc, vc, k, v, *, tq, tk, extras=(), sink=None, lam_init=0.0):
    b, _, t, _ = q.shape
    l_ctx = kc.shape[2]
    has_x = k is not None
    gq = 2 if mode == "B" else 1
    shared_kv = mode == "C"

    def kv_idx(bi, g, i):
        return (bi, g // 2 if shared_kv else g, 0, 0)

    in_specs = [
        pl.BlockSpec((None, gq, tq, LANES), lambda bi, g, i: (bi, g, i, 0)),
        pl.BlockSpec((None, gq, l_ctx, LANES), kv_idx),
        pl.BlockSpec((None, 1, l_ctx, LANES), kv_idx),
    ]
    args = [q, kc, vc]
    if has_x:
        n = k.shape[2]
        in_specs += [pl.BlockSpec((None, gq, n, LANES), kv_idx), pl.BlockSpec((None, 1, n, LANES), kv_idx)]
        args += [k, v]
    for e in extras:
        in_specs.append(pl.BlockSpec(e.shape, lambda bi, g, i, nd=e.ndim: (0,) * nd))
        args.append(e)
    if sink is not None:
        in_specs.append(pl.BlockSpec(memory_space=pltpu.SMEM))
        args.append(sink)
    state = pltpu.VMEM((2, tq, LANES), f32)
    width = max(tk, l_ctx) if has_x else l_ctx
    if has_x:
        assert k.shape[2] % tk == 0 and tk % LANES == 0
    s_buf, p_buf = pltpu.VMEM((2 * tq, width), f32), pltpu.VMEM((2 * tq, width), bf16)
    scratch = [pltpu.VMEM((2 * tq, LANES), bf16), state, pltpu.VMEM((2, tq, 2 * LANES), f32),
               s_buf, s_buf, p_buf, p_buf, state, state]
    return pl.pallas_call(
        functools.partial(_flash_kernel, mode=mode, has_x=has_x, has_sink=sink is not None, tk=tk,
                          lam_init=lam_init),
        name="flash_" + mode + ("" if has_x else "_ctx"),
        grid=(b, 4, t // tq),
        in_specs=in_specs,
        out_specs=pl.BlockSpec((None, tq, LANES), lambda bi, g, i: (bi, i, g)),
        out_shape=jax.ShapeDtypeStruct((b, t, 4 * LANES), bf16),
        scratch_shapes=scratch,
        compiler_params=pltpu.CompilerParams(vmem_limit_bytes=VMEM_LIMIT),
    )(*args)


def _window_kernel(q_ref, kc_ref, vc_ref, k_ref, v_ref, sink_ref, o_ref):
    tq = q_ref.shape[1]
    n = k_ref.shape[1]
    l_ctx = kc_ref.shape[1]
    band = tq + 2 * WINDOW
    width = l_ctx + band
    t0 = pl.program_id(1) * tq
    start = pl.multiple_of(jnp.clip(t0 - WINDOW, 0, n - band), WINDOW)
    col = lax.broadcasted_iota(jnp.int32, (tq, width), 1)
    qpos = t0 + lax.broadcasted_iota(jnp.int32, (tq, width), 0)
    valid = (col < l_ctx) | (jnp.abs(col - l_ctx + start - qpos) <= WINDOW)
    lane = lax.broadcasted_iota(jnp.int32, (tq, LANES), 1)
    lo = lane < HALF
    for j in range(4):
        if j % 2 == 0:
            kv = j // 2
            k_all = jnp.concatenate([kc_ref[kv], k_ref[kv, pl.ds(start, band), :]], axis=0)
            v_all = jnp.concatenate([vc_ref[kv], v_ref[kv, pl.ds(start, band), :]], axis=0)
            v_ext = jnp.concatenate([v_all, jnp.ones_like(v_all)], axis=1)
        q = q_ref[j]
        zero = jnp.zeros_like(q)
        s = _dot_nt(jnp.concatenate([jnp.where(lo, q, zero), jnp.where(lo, zero, q)], axis=0), k_all)
        outs = []
        for u in range(2):
            sink = sink_ref[2 * j + u] * LOG2E
            su = jnp.where(valid, s[u * tq:(u + 1) * tq], NEG_INF)
            cols = [su[:, c * LANES:(c + 1) * LANES] for c in range(width // LANES)]
            m = jnp.maximum(jnp.max(functools.reduce(jnp.maximum, cols), axis=-1, keepdims=True), sink)
            p = jnp.concatenate([jnp.exp2(c - m).astype(bf16) for c in cols], axis=1)
            acc = _dot(p, v_ext)
            outs.append(acc[:, :LANES] / (acc[:, LANES:] + jnp.exp2(sink - m)))
        o_ref[:, j * LANES:(j + 1) * LANES] = jnp.where(lo, outs[0], outs[1]).astype(bf16)


def _window_call(q, kc, vc, k, v, sink, *, tq):
    b, _, n, _ = q.shape
    l_ctx = kc.shape[2]
    return pl.pallas_call(
        _window_kernel,
        name="window_attn",
        grid=(b, n // tq),
        in_specs=[
            pl.BlockSpec((None, 4, tq, LANES), lambda bi, i: (bi, 0, i, 0)),
            pl.BlockSpec((None, 2, l_ctx, LANES), lambda bi, i: (bi, 0, 0, 0)),
            pl.BlockSpec((None, 2, l_ctx, LANES), lambda bi, i: (bi, 0, 0, 0)),
            pl.BlockSpec((None, 2, n, LANES), lambda bi, i: (bi, 0, 0, 0)),
            pl.BlockSpec((None, 2, n, LANES), lambda bi, i: (bi, 0, 0, 0)),
            pl.BlockSpec(memory_space=pltpu.SMEM),
        ],
        out_specs=pl.BlockSpec((None, tq, 4 * LANES), lambda bi, i: (bi, i, 0)),
        out_shape=jax.ShapeDtypeStruct((b, n, 4 * LANES), bf16),
        compiler_params=pltpu.CompilerParams(vmem_limit_bytes=VMEM_LIMIT),
    )(q, kc, vc, k, v, sink)


def _merge_kernel(x_ref, mod_ref, oa_ref, ob_ref, oc_ref, od_ref, wg_ref, wbr_ref, wout_ref, o_ref):
    x = x_ref[...]
    h = _modulate(x, mod_ref[0:1, :], mod_ref[1:2, :]).astype(bf16)
    mix = None
    for k, br_ref in enumerate((oa_ref, ob_ref, oc_ref, od_ref)):
        g = _dot(h, wg_ref[:, k * D_MODEL:(k + 1) * D_MODEL])
        y = _dot(br_ref[...], wbr_ref[k])
        term = _sigmoid(g) * y
        mix = term if mix is None else mix + term
    o_ref[...] = x + mod_ref[2:3, :] * _dot(mix.astype(bf16), wout_ref[...])


def _merge_call(xs, mod, branches, w, tm):
    b, t, d = xs.shape
    tiles = t // tm
    x2 = xs.reshape(b * t, d)
    br = [o.reshape(b * t, BRANCH_WIDTH) for o in branches]
    br_spec = pl.BlockSpec((tm, BRANCH_WIDTH), lambda i: (i, 0))
    out = pl.pallas_call(
        _merge_kernel,
        name="merge",
        grid=(b * tiles,),
        in_specs=[
            pl.BlockSpec((tm, d), lambda i: (i, 0)),
            pl.BlockSpec((None, N_MOD, d), lambda i: (i // tiles, 0, 0)),
            br_spec, br_spec, br_spec, br_spec,
            _layer_spec(w["wg"], w["l"]), _layer_spec(w["wbr"], w["l"]), _layer_spec(w["wout"], w["l"]),
        ],
        out_specs=pl.BlockSpec((tm, d), lambda i: (i, 0)),
        out_shape=jax.ShapeDtypeStruct((b * t, d), f32),
        compiler_params=pltpu.CompilerParams(vmem_limit_bytes=VMEM_LIMIT),
    )(x2, mod, *br, w["wg"], w["wbr"], w["wout"])
    return out.reshape(b, t, d)


def _ffn_kernel(x_ref, xp_ref, xn_ref, mod_ref, wup_ref, cw_ref, cb_ref, wdn_ref, fin_ref, o_ref, u_ref, *,
                seq_len, final_norm):
    tm = x_ref.shape[0]
    x = x_ref[...]
    shift, scale = mod_ref[3:4, :], mod_ref[4:5, :]
    tile = pl.program_id(0) % (seq_len // tm)
    keep_prev = (tile != 0).astype(f32)
    keep_next = (tile != seq_len // tm - 1).astype(f32)
    h = jnp.concatenate([
        (_modulate(xp_ref[...], shift, scale) * keep_prev).astype(bf16),
        _modulate(x, shift, scale).astype(bf16),
        (_modulate(xn_ref[...], shift, scale) * keep_next).astype(bf16)], axis=0)

    def conv(c0, width):
        cols = slice(c0, c0 + width)
        u_ref[:, cols] = _dot(h, wup_ref[:, cols])
        prev = u_ref[HALO - 1:HALO - 1 + tm, cols]
        cur = u_ref[HALO:HALO + tm, cols]
        nxt = u_ref[HALO + 1:HALO + 1 + tm, cols]
        return prev * cw_ref[0:1, cols] + cur * cw_ref[1:2, cols] + nxt * cw_ref[2:3, cols] + cb_ref[:, cols]

    acc = None
    cv = 0
    for width in FFN_CHUNKS:
        val = conv(cv, width)
        gate = conv(D_FF + cv, width)
        a = (gate * _sigmoid(gate) * val).astype(bf16)
        part = _dot(a, wdn_ref[cv:cv + width, :])
        acc = part if acc is None else acc + part
        cv += width
    out = x + mod_ref[5:6, :] * acc
    if final_norm:
        out = _rms(out) * fin_ref[...]
    o_ref[...] = out


def _ffn_call(xs, mod, w, fin, tm, final_norm):
    b, t, d = xs.shape
    tiles = t // tm
    x2 = xs.reshape(b * t, d)
    hb = tm // HALO
    last_halo = b * t // HALO - 1
    out = pl.pallas_call(
        functools.partial(_ffn_kernel, seq_len=t, final_norm=final_norm),
        name="conv_ffn",
        grid=(b * tiles,),
        in_specs=[
            pl.BlockSpec((tm, d), lambda i: (i, 0)),
            pl.BlockSpec((HALO, d), lambda i: (jnp.maximum(i * hb - 1, 0), 0)),
            pl.BlockSpec((HALO, d), lambda i: (jnp.minimum((i + 1) * hb, last_halo), 0)),
            pl.BlockSpec((None, N_MOD, d), lambda i: (i // tiles, 0, 0)),
            _layer_spec(w["wup"], w["l"]), _layer_spec(w["cw"], w["l"]), _layer_spec(w["cb"], w["l"]),
            _layer_spec(w["wdn"], w["l"]), _const_spec((1, d)),
        ],
        out_specs=pl.BlockSpec((tm, d), lambda i: (i, 0)),
        out_shape=jax.ShapeDtypeStruct((b * t, d), f32),
        scratch_shapes=[pltpu.VMEM((tm + 2 * HALO, 2 * D_FF), f32)],
        compiler_params=pltpu.CompilerParams(vmem_limit_bytes=VMEM_LIMIT),
    )(x2, x2, x2, mod, w["wup"], w["cw"], w["cb"], w["wdn"], fin)
    return out.reshape(b, t, d)


def _layer_weights(l, stacked, mla_q_norm, mla_kv_norm, mla_w_uq, mla_w_ukv, gqa_q_norm, gqa_k_norm):
    hq = MLA_NOPE_DIM + MLA_ROPE_DIM
    wuq = mla_w_uq[l].reshape(MLA_Q_LORA, MLA_HEADS, hq)
    wuq = jnp.pad(wuq, ((0, 0), (0, 0), (0, LANES - hq))).reshape(MLA_Q_LORA, MLA_HEADS * LANES).astype(bf16)
    wukv = mla_w_ukv[l].reshape(MLA_KV_LORA, MLA_HEADS, MLA_NOPE_DIM + MLA_V_DIM)
    wk = jnp.pad(wukv[:, :, :MLA_NOPE_DIM], ((0, 0), (0, 0), (0, LANES - MLA_NOPE_DIM)))
    wk = wk.reshape(MLA_KV_LORA, MLA_HEADS * LANES).astype(bf16)
    wv = wukv[:, :, MLA_NOPE_DIM:].reshape(MLA_KV_LORA, MLA_HEADS * MLA_V_DIM).astype(bf16)
    return dict(
        stacked, l=l, wuq=wuq, wk=wk, wv=wv,
        qn=mla_q_norm[l].reshape(1, -1), kvn=mla_kv_norm[l].reshape(1, -1),
        gq=jnp.tile(gqa_q_norm[l], 2).reshape(1, LANES), gk=jnp.tile(gqa_k_norm[l], 2).reshape(1, LANES))


def _rope_tables(n, l_ctx):
    t = jnp.arange(n, dtype=jnp.int32)
    row, col = (t // GRID_W).astype(f32), (t % GRID_W).astype(f32)

    def expanded(rot_dim):
        axis_dim = rot_dim // 2
        inv = ROPE_THETA ** (-jnp.arange(0, axis_dim, 2, dtype=f32) / axis_dim)
        ang = jnp.concatenate([row[:, None] * inv, col[:, None] * inv], axis=-1)
        cos = jnp.repeat(jnp.cos(ang), 2, axis=-1)
        sin = jnp.repeat(jnp.sin(ang), 2, axis=-1) * jnp.tile(jnp.array([-1.0, 1.0], f32), rot_dim // 2)
        return cos, sin

    c64, s64 = expanded(2 * HALF // 2)
    c64, s64 = jnp.tile(c64, (1, 2)), jnp.tile(s64, (1, 2))
    c32, s32 = expanded(MLA_ROPE_DIM)
    pad = LANES - MLA_NOPE_DIM - MLA_ROPE_DIM
    cb = jnp.concatenate([jnp.ones((n, MLA_NOPE_DIM), f32), c32, jnp.ones((n, pad), f32)], axis=1)
    sb = jnp.concatenate([jnp.zeros((n, MLA_NOPE_DIM), f32), s32, jnp.zeros((n, pad), f32)], axis=1)
    ones, zeros = jnp.ones((l_ctx, LANES), f32), jnp.zeros((l_ctx, LANES), f32)
    return (c64, s64, cb, sb), (ones, zeros, ones, zeros)


def kernel(x, c, ctx, c_ctx, w_mod, b_mod, w_in, diff_lambda, diff_subln, mla_q_norm, mla_kv_norm, mla_w_uq,
           mla_w_ukv, gqa_q_norm, gqa_k_norm, swa_sink, w_branch, w_out, ffn_w_up, ffn_conv_w, ffn_conv_b,
           ffn_w_down, final_norm):
    b, n, d = x.shape
    l_ctx = ctx.shape[1]
    depth = w_mod.shape[0]
    tm_x = min(512, n)
    tq = min(512, n)
    tk = min(1024, n)

    cs = jnp.zeros((8, d), f32).at[:b].set(c).at[b].set(c_ctx)
    mods = _mod_call(cs, w_mod, b_mod).reshape(depth, 8, N_MOD, d)
    tab_x, tab_c = _rope_tables(n, l_ctx)
    fin = final_norm.reshape(1, d)

    w1, wg = _repack_call(w_in)
    stacked = dict(w1=w1, wg=wg, wbr=w_branch.astype(bf16), wout=w_out.astype(bf16),
                   wup=ffn_w_up.astype(bf16), wdn=ffn_w_down.astype(bf16), cw=ffn_conv_w,
                   cb=ffn_conv_b.reshape(depth, 1, -1))

    xs, xc = x, ctx
    for l in range(depth):
        w = _layer_weights(l, stacked, mla_q_norm, mla_kv_norm, mla_w_uq, mla_w_ukv, gqa_q_norm, gqa_k_norm)
        mod_x = mods[l, :b]
        mod_c = jnp.broadcast_to(mods[l, b][None], (b, N_MOD, d))
        lam_init = 0.8 - 0.6 * math.exp(-0.3 * l)
        a_extras = (diff_lambda[l], diff_subln[l].reshape(1, LANES))
        sink = swa_sink[l]
        update_ctx = l < depth - 1

        px = _proj_call(xs, mod_x, tab_x, w, tm_x)
        pc = _proj_call(xc, mod_c, tab_c, w, l_ctx)
        qa, ka, va, qb, kb, vb, qc, kc, vc, qd, kd, vd = px
        qa_c, ka_c, va_c, qb_c, kb_c, vb_c, qc_c, kc_c, vc_c, qd_c, kd_c, vd_c = pc

        oa = _flash_call("A", qa, ka_c, va_c, ka, va, tq=tq, tk=tk, extras=a_extras, lam_init=lam_init)
        ob = _flash_call("B", qb, kb_c, vb_c, kb, vb, tq=tq, tk=tk)
        oc = _flash_call("C", qc, kc_c, vc_c, kc, vc, tq=tq, tk=tk)
        od = _window_call(qd, kd_c, vd_c, kd, vd, sink, tq=min(256, n))
        xs_mid = _merge_call(xs, mod_x, (oa, ob, oc, od), w, tm_x)
        xs_new = _ffn_call(xs_mid, mod_x, w, fin, tm_x, final_norm=not update_ctx)

        if update_ctx:
            oa = _flash_call("A", qa_c, ka_c, va_c, None, None, tq=l_ctx, tk=tk, extras=a_extras,
                             lam_init=lam_init)
            ob = _flash_call("B", qb_c, kb_c, vb_c, None, None, tq=l_ctx, tk=tk)
            oc = _flash_call("C", qc_c, kc_c, vc_c, None, None, tq=l_ctx, tk=tk)
            od = _flash_call("C", qd_c, kd_c, vd_c, None, None, tq=l_ctx, tk=tk, sink=sink)
            xc_mid = _merge_call(xc, mod_c, (oa, ob, oc, od), w, l_ctx)
            xc = _ffn_call(xc_mid, mod_c, w, fin, l_ctx, final_norm=False)
        xs = xs_new
    return xs
```

```python
import functools
import math

import jax
import jax.numpy as jnp
from jax import lax
from jax.experimental import pallas as pl
from jax.experimental.pallas import tpu as pltpu

f32 = jnp.float32
bf16 = jnp.bfloat16

LANES = 128
HALF = 64
D_MODEL = 1024
GRID_W = 64
WINDOW = 128
N_BRANCH = 4
N_MOD = 6
EPS = 1e-6
ROPE_THETA = 10000.0
NEG_INF = -1e30
LOG2E = math.log2(math.e)
DIFF_HEADS = 4
DIFF_QK_DIM = 64
MLA_HEADS = 8
MLA_Q_LORA = 256
MLA_KV_LORA = 256
MLA_NOPE_DIM = 64
MLA_ROPE_DIM = 32
MLA_V_DIM = 64
GQA_HEADS = 8
BRANCH_WIDTH = 512
D_FF = 2816
FFN_CHUNKS = (1536, 1280)
assert sum(FFN_CHUNKS) == D_FF
HALO = 8
VMEM_LIMIT = 56 * 1024 * 1024

IN_WIDTH = 7712
ALIGNED_COLS = 2048
ROT_COLS = 32
W1_WIDTH = 4224
GATE_WIDTH = N_BRANCH * D_MODEL


def _dot(a, b):
    return jnp.dot(a, b, preferred_element_type=f32)


def _dot_nt(a, b):
    return lax.dot_general(a, b, (((1,), (1,)), ((), ())), preferred_element_type=f32)


def _sigmoid(v):
    return 1.0 / (1.0 + jnp.exp(-v))


def _rms(v):
    return v * lax.rsqrt(jnp.mean(v * v, axis=-1, keepdims=True) + EPS)


def _modulate(v, shift, scale):
    return _rms(v) * (1.0 + scale) + shift


def _const_spec(shape):
    zeros = (0,) * len(shape)
    return pl.BlockSpec(shape, lambda *_: zeros, pipeline_mode=pl.Buffered(1))


def _layer_spec(stacked, l):
    tail = stacked.shape[1:]
    idx = (l,) + (0,) * len(tail)
    return pl.BlockSpec((None,) + tail, lambda *_: idx, pipeline_mode=pl.Buffered(1))


def _mod_kernel(c_ref, w_ref, b_ref, o_ref):
    cv = c_ref[...]
    a = cv * _sigmoid(cv)
    o_ref[...] = _dot(a, w_ref[...]) + b_ref[...]


def _mod_call(cs, w_mod, b_mod):
    depth, d, width = w_mod.shape
    tn = 1536
    return pl.pallas_call(
        _mod_kernel,
        name="mod_vectors",
        grid=(depth, width // tn),
        in_specs=[
            pl.BlockSpec((8, d), lambda l, j: (0, 0)),
            pl.BlockSpec((None, d, tn), lambda l, j: (l, 0, j)),
            pl.BlockSpec((None, 1, tn), lambda l, j: (l, 0, j)),
        ],
        out_specs=pl.BlockSpec((None, 8, tn), lambda l, j: (l, 0, j)),
        out_shape=jax.ShapeDtypeStruct((depth, 8, width), f32),
        compiler_params=pltpu.CompilerParams(vmem_limit_bytes=VMEM_LIMIT),
    )(cs, w_mod, b_mod.reshape(depth, 1, width))


def _repack_kernel(wt_ref, w1_ref, wg_ref):
    tr = wt_ref.shape[1]

    def rows(r0, n):
        return wt_ref[r0:r0 + n, :]

    def emit(dst_ref, out, block):
        dst_ref[:, out:out + LANES] = block.T.astype(bf16)

    for c in range(ALIGNED_COLS // LANES):
        emit(w1_ref, c * LANES, rows(c * LANES, LANES))
    pad = LANES - MLA_NOPE_DIM - ROT_COLS
    emit(w1_ref, ALIGNED_COLS, jnp.concatenate(
        [jnp.zeros((MLA_NOPE_DIM, tr), f32), rows(ALIGNED_COLS, ROT_COLS), jnp.zeros((pad, tr), f32)], axis=0))

    out = ALIGNED_COLS + LANES
    src = ALIGNED_COLS + ROT_COLS
    for _ in range(2):
        for _ in range(4):
            emit(w1_ref, out, rows(src, LANES))
            out, src = out + LANES, src + LANES
        for _ in range(2):
            for kv in range(2):
                half = rows(src + kv * HALF, HALF)
                emit(w1_ref, out, jnp.concatenate([half, half], axis=0))
                out += LANES
            src += LANES
    assert out == W1_WIDTH and IN_WIDTH - src == GATE_WIDTH
    for j in range(GATE_WIDTH // LANES):
        emit(wg_ref, j * LANES, rows(src + j * LANES, LANES))


def _repack_call(w_in):
    depth, d, width = w_in.shape
    assert width == IN_WIDTH
    tr = 256
    w_t = jnp.swapaxes(w_in, 1, 2)
    return pl.pallas_call(
        _repack_kernel,
        name="repack_w_in",
        grid=(depth, d // tr),
        in_specs=[pl.BlockSpec((None, width, tr), lambda l, i: (l, 0, i))],
        out_specs=[pl.BlockSpec((None, tr, W1_WIDTH), lambda l, i: (l, i, 0)),
                   pl.BlockSpec((None, tr, GATE_WIDTH), lambda l, i: (l, i, 0))],
        out_shape=[jax.ShapeDtypeStruct((depth, d, W1_WIDTH), bf16),
                   jax.ShapeDtypeStruct((depth, d, GATE_WIDTH), bf16)],
        compiler_params=pltpu.CompilerParams(vmem_limit_bytes=VMEM_LIMIT),
    )(w_t)


def _proj_kernel(x_ref, mod_ref, c64_ref, s64_ref, cb_ref, sb_ref, w1_ref, qn_ref, kvn_ref, wuq_ref, wk_ref,
                 wv_ref, gq_ref, gk_ref,
                 qa_ref, ka_ref, va_ref, qb_ref, kb_ref, vb_ref, qc_ref, kc_ref, vc_ref, qd_ref, kd_ref, vd_ref,
                 *, scale_b):
    tm = x_ref.shape[0]
    h = _modulate(x_ref[...], mod_ref[0:1, :], mod_ref[1:2, :]).astype(bf16)
    lane = lax.broadcasted_iota(jnp.int32, (tm, LANES), 1)
    even = (lane & 1) == 0
    lo = lane < HALF
    c64, s64, cb, sb = c64_ref[...], s64_ref[...], cb_ref[...], sb_ref[...]

    def rope(v, cos, sin):
        partner = jnp.where(even, pltpu.roll(v, LANES - 1, 1), pltpu.roll(v, 1, 1))
        return v * cos + partner * sin

    def head_norm(v, gain):
        sq = v * v
        s_lo = jnp.sum(jnp.where(lo, sq, 0.0), axis=-1, keepdims=True)
        s_hi = jnp.sum(jnp.where(lo, 0.0, sq), axis=-1, keepdims=True)
        inv = jnp.where(lo, lax.rsqrt(s_lo * (1.0 / HALF) + EPS), lax.rsqrt(s_hi * (1.0 / HALF) + EPS))
        return v * inv * gain

    def chunk(z, j):
        return z[:, j * LANES:(j + 1) * LANES]

    qk_scale = DIFF_QK_DIM ** -0.5 * LOG2E

    z = _dot(h, w1_ref[:, 0:1536])
    for j in range(4):
        qa_ref[j] = (rope(chunk(z, j), c64, s64) * qk_scale).astype(bf16)
        ka_ref[j] = rope(chunk(z, 4 + j), c64, s64).astype(bf16)
        va_ref[j] = chunk(z, 8 + j).astype(bf16)

    z = _dot(h, w1_ref[:, 1536:2176])
    dq = (_rms(z[:, 0:256]) * qn_ref[...]).astype(bf16)
    ckv = (_rms(z[:, 256:512]) * kvn_ref[...]).astype(bf16)
    k_rot = rope(z[:, 512:640], cb, sb)
    zq = _dot(dq, wuq_ref[...])
    zk = _dot(ckv, wk_ref[...])
    zv = _dot(ckv, wv_ref[...])
    for j in range(MLA_HEADS):
        qb_ref[j] = (rope(chunk(zq, j), cb, sb) * scale_b).astype(bf16)
        kb_ref[j] = (chunk(zk, j) + k_rot).astype(bf16)
    for j in range(4):
        vb_ref[j] = chunk(zv, j).astype(bf16)

    z = _dot(h, w1_ref[:, 2176:3200])
    gq, gk = gq_ref[...], gk_ref[...]
    for j in range(4):
        qc_ref[j] = (rope(head_norm(chunk(z, j), gq), c64, s64) * qk_scale).astype(bf16)
    for j in range(2):
        kc_ref[j] = rope(head_norm(chunk(z, 4 + j), gk), c64, s64).astype(bf16)
        vc_ref[j] = chunk(z, 6 + j).astype(bf16)

    z = _dot(h, w1_ref[:, 3200:4224])
    for j in range(4):
        qd_ref[j] = (rope(chunk(z, j), c64, s64) * qk_scale).astype(bf16)
    for j in range(2):
        kd_ref[j] = rope(chunk(z, 4 + j), c64, s64).astype(bf16)
        vd_ref[j] = chunk(z, 6 + j).astype(bf16)


def _proj_call(xs, mod, tables, w, tm):
    b, t, d = xs.shape
    c64, s64, cb, sb = tables
    grid = (b, t // tm)

    def out(nchunks):
        return (jax.ShapeDtypeStruct((b, nchunks, t, LANES), bf16),
                pl.BlockSpec((None, nchunks, tm, LANES), lambda bi, i: (bi, 0, i, 0)))

    outs = [out(n) for n in (4, 4, 4, 8, 8, 4, 4, 2, 2, 4, 2, 2)]
    tab_spec = pl.BlockSpec((tm, LANES), lambda bi, i: (i, 0))
    scale_b = (MLA_NOPE_DIM + MLA_ROPE_DIM) ** -0.5 * LOG2E
    return pl.pallas_call(
        functools.partial(_proj_kernel, scale_b=scale_b),
        name="in_proj",
        grid=grid,
        in_specs=[
            pl.BlockSpec((None, tm, d), lambda bi, i: (bi, i, 0)),
            pl.BlockSpec((None, N_MOD, d), lambda bi, i: (bi, 0, 0)),
            tab_spec, tab_spec, tab_spec, tab_spec,
            _layer_spec(w["w1"], w["l"]), _const_spec((1, 256)), _const_spec((1, 256)),
            _const_spec(w["wuq"].shape), _const_spec(w["wk"].shape), _const_spec(w["wv"].shape),
            _const_spec((1, LANES)), _const_spec((1, LANES)),
        ],
        out_specs=[o[1] for o in outs],
        out_shape=[o[0] for o in outs],
        compiler_params=pltpu.CompilerParams(vmem_limit_bytes=VMEM_LIMIT),
    )(xs, mod, c64, s64, cb, sb, w["w1"], w["qn"], w["kvn"], w["wuq"], w["wk"], w["wv"], w["gq"], w["gk"])


def _flash_kernel(*refs, mode, has_x, has_sink, tk, lam_init):
    refs = list(refs)
    q_ref, kc_ref, vc_ref = refs[:3]
    pos = 3
    if has_x:
        k_ref, v_ref = refs[pos:pos + 2]
        pos += 2
    if mode == "A":
        dl_ref, subln_ref = refs[pos:pos + 2]
        pos += 2
    if has_sink:
        sink_ref = refs[pos]
        pos += 1
    o_ref, q2_ref, m_ref, acc_ref = refs[pos:pos + 4]
    s_refs, p_refs, al_refs = refs[pos + 4:pos + 6], refs[pos + 6:pos + 8], refs[pos + 8:pos + 10]

    tq = q_ref.shape[1]
    lane = lax.broadcasted_iota(jnp.int32, (tq, LANES), 1)
    lo = lane < HALF
    shared_k = mode != "B"
    if shared_k:
        q = q_ref[0]
        zero = jnp.zeros_like(q)
        q2_ref[0:tq, :] = jnp.where(lo, q, zero)
        q2_ref[tq:2 * tq, :] = jnp.where(lo, zero, q)
    else:
        q2_ref[0:tq, :] = q_ref[0]
        q2_ref[tq:2 * tq, :] = q_ref[1]

    for u in range(2):
        if has_sink:
            sink = sink_ref[2 * pl.program_id(1) + u] * LOG2E
            m_ref[u] = jnp.full((tq, LANES), sink, f32)
            acc_ref[u, :, LANES:] = jnp.ones((tq, LANES), f32)
        else:
            m_ref[u] = jnp.full((tq, LANES), NEG_INF, f32)
            acc_ref[u, :, LANES:] = jnp.zeros((tq, LANES), f32)
        acc_ref[u, :, :LANES] = jnp.zeros((tq, LANES), f32)

    def scores(chunk, s_ref):
        kref, _, start, size = chunk
        if shared_k:
            s_ref[:, :size] = _dot_nt(q2_ref[...], kref[0, start:start + size, :])
        else:
            for u in range(2):
                rows = slice(u * tq, (u + 1) * tq)
                s_ref[rows, :size] = _dot_nt(q2_ref[rows, :], kref[u, start:start + size, :])

    def softmax(chunk, s_ref, p_ref, al_ref):
        size = chunk[3]
        for u in range(2):
            rows = slice(u * tq, (u + 1) * tq)
            cols = [s_ref[rows, j * LANES:(j + 1) * LANES] for j in range(size // LANES)]
            m_prev = m_ref[u]
            m_new = jnp.maximum(m_prev, jnp.max(functools.reduce(jnp.maximum, cols), axis=-1, keepdims=True))
            al_ref[u] = jnp.exp2(m_prev - m_new)
            for j, cj in enumerate(cols):
                p_ref[rows, j * LANES:(j + 1) * LANES] = jnp.exp2(cj - m_new).astype(bf16)
            m_ref[u] = m_new

    def pv(chunk, p_ref, al_ref):
        _, vref, start, size = chunk
        vext = jnp.concatenate([vref[0, start:start + size, :], jnp.ones((size, LANES), bf16)], axis=1)
        for u in range(2):
            alpha = al_ref[u]
            alpha2 = jnp.concatenate([alpha, alpha], axis=1)
            acc_ref[u] = alpha2 * acc_ref[u] + _dot(p_ref[u * tq:(u + 1) * tq, :size], vext)

    chunks = [(kc_ref, vc_ref, 0, kc_ref.shape[1])]
    if has_x:
        chunks += [(k_ref, v_ref, c * tk, tk) for c in range(k_ref.shape[1] // tk)]

    for i in range(len(chunks) + 2):
        if i < len(chunks):
            scores(chunks[i], s_refs[i % 2])
        if 1 <= i <= len(chunks):
            softmax(chunks[i - 1], s_refs[(i - 1) % 2], p_refs[(i - 1) % 2], al_refs[(i - 1) % 2])
        if i >= 2:
            pv(chunks[i - 2], p_refs[i % 2], al_refs[i % 2])

    o0 = acc_ref[0, :, :LANES] / acc_ref[0, :, LANES:]
    o1 = acc_ref[1, :, :LANES] / acc_ref[1, :, LANES:]
    if mode == "A":
        dl = dl_ref[...]
        lam = (jnp.exp(jnp.sum(dl[0:1] * dl[1:2], axis=-1, keepdims=True))
               - jnp.exp(jnp.sum(dl[2:3] * dl[3:4], axis=-1, keepdims=True)) + lam_init)
        o = _rms(o0 - lam * o1) * subln_ref[...] * (1.0 - lam_init)
    else:
        o = jnp.where(lo, o0, o1)
    o_ref[...] = o.astype(bf16)


def _flash_call(mode, q, kc, vc, k, v, *, tq, tk, extras=(), sink=None, lam_init=0.0):
    b, _, t, _ = q.shape
    l_ctx = kc.shape[2]
    has_x = k is not None
    gq = 2 if mode == "B" else 1
    shared_kv = mode == "C"

    def kv_idx(bi, g, i):
        return (bi, g // 2 if shared_kv else g, 0, 0)

    in_specs = [
        pl.BlockSpec((None, gq, tq, LANES), lambda bi, g, i: (bi, g, i, 0)),
        pl.BlockSpec((None, gq, l_ctx, LANES), kv_idx),
        pl.BlockSpec((None, 1, l_ctx, LANES), kv_idx),
    ]
    args = [q, kc, vc]
    if has_x:
        n = k.shape[2]
        in_specs += [pl.BlockSpec((None, gq, n, LANES), kv_idx), pl.BlockSpec((None, 1, n, LANES), kv_idx)]
        args += [k, v]
    for e in extras:
        in_specs.append(pl.BlockSpec(e.shape, lambda bi, g, i, nd=e.ndim: (0,) * nd))
        args.append(e)
    if sink is not None:
        in_specs.append(pl.BlockSpec(memory_space=pltpu.SMEM))
        args.append(sink)
    state = pltpu.VMEM((2, tq, LANES), f32)
    width = max(tk, l_ctx) if has_x else l_ctx
    if has_x:
        assert k.shape[2] % tk == 0 and tk % LANES == 0
    s_buf, p_buf = pltpu.VMEM((2 * tq, width), f32), pltpu.VMEM((2 * tq, width), bf16)
    scratch = [pltpu.VMEM((2 * tq, LANES), bf16), state, pltpu.VMEM((2, tq, 2 * LANES), f32),
               s_buf, s_buf, p_buf, p_buf, state, state]
    return pl.pallas_call(
        functools.partial(_flash_kernel, mode=mode, has_x=has_x, has_sink=sink is not None, tk=tk,
                          lam_init=lam_init),
        name="flash_" + mode + ("" if has_x else "_ctx"),
        grid=(b, 4, t // tq),
        in_specs=in_specs,
        out_specs=pl.BlockSpec((None, tq, LANES), lambda bi, g, i: (bi, i, g)),
        out_shape=jax.ShapeDtypeStruct((b, t, 4 * LANES), bf16),
        scratch_shapes=scratch,
        compiler_params=pltpu.CompilerParams(vmem_limit_bytes=VMEM_LIMIT),
    )(*args)


def _window_kernel(q_ref, kc_ref, vc_ref, k_ref, v_ref, sink_ref, o_ref):
    tq = q_ref.shape[1]
    n = k_ref.shape[1]
    l_ctx = kc_ref.shape[1]
    band = tq + 2 * WINDOW
    width = l_ctx + band
    t0 = pl.program_id(1) * tq
    start = pl.multiple_of(jnp.clip(t0 - WINDOW, 0, n - band), WINDOW)
    col = lax.broadcasted_iota(jnp.int32, (tq, width), 1)
    qpos = t0 + lax.broadcasted_iota(jnp.int32, (tq, width), 0)
    valid = (col < l_ctx) | (jnp.abs(col - l_ctx + start - qpos) <= WINDOW)
    lane = lax.broadcasted_iota(jnp.int32, (tq, LANES), 1)
    lo = lane < HALF
    for j in range(4):
        if j % 2 == 0:
            kv = j // 2
            k_all = jnp.concatenate([kc_ref[kv], k_ref[kv, pl.ds(start, band), :]], axis=0)
            v_all = jnp.concatenate([vc_ref[kv], v_ref[kv, pl.ds(start, band), :]], axis=0)
            v_ext = jnp.concatenate([v_all, jnp.ones_like(v_all)], axis=1)
        q = q_ref[j]
        zero = jnp.zeros_like(q)
        s = _dot_nt(jnp.concatenate([jnp.where(lo, q, zero), jnp.where(lo, zero, q)], axis=0), k_all)
        outs = []
        for u in range(2):
            sink = sink_ref[2 * j + u] * LOG2E
            su = jnp.where(valid, s[u * tq:(u + 1) * tq], NEG_INF)
            cols = [su[:, c * LANES:(c + 1) * LANES] for c in range(width // LANES)]
            m = jnp.maximum(jnp.max(functools.reduce(jnp.maximum, cols), axis=-1, keepdims=True), sink)
            p = jnp.concatenate([jnp.exp2(c - m).astype(bf16) for c in cols], axis=1)
            acc = _dot(p, v_ext)
            outs.append(acc[:, :LANES] / (acc[:, LANES:] + jnp.exp2(sink - m)))
        o_ref[:, j * LANES:(j + 1) * LANES] = jnp.where(lo, outs[0], outs[1]).astype(bf16)


def _window_call(q, kc, vc, k, v, sink, *, tq):
    b, _, n, _ = q.shape
    l_ctx = kc.shape[2]
    return pl.pallas_call(
        _window_kernel,
        name="window_attn",
        grid=(b, n // tq),
        in_specs=[
            pl.BlockSpec((None, 4, tq, LANES), lambda bi, i: (bi, 0, i, 0)),
            pl.BlockSpec((None, 2, l_ctx, LANES), lambda bi, i: (bi, 0, 0, 0)),
            pl.BlockSpec((None, 2, l_ctx, LANES), lambda bi, i: (bi, 0, 0, 0)),
            pl.BlockSpec((None, 2, n, LANES), lambda bi, i: (bi, 0, 0, 0)),
            pl.BlockSpec((None, 2, n, LANES), lambda bi, i: (bi, 0, 0, 0)),
            pl.BlockSpec(memory_space=pltpu.SMEM),
        ],
        out_specs=pl.BlockSpec((None, tq, 4 * LANES), lambda bi, i: (bi, i, 0)),
        out_shape=jax.ShapeDtypeStruct((b, n, 4 * LANES), bf16),
        compiler_params=pltpu.CompilerParams(vmem_limit_bytes=VMEM_LIMIT),
    )(q, kc, vc, k, v, sink)


def _merge_kernel(x_ref, mod_ref, oa_ref, ob_ref, oc_ref, od_ref, wg_ref, wbr_ref, wout_ref, o_ref):
    x = x_ref[...]
    h = _modulate(x, mod_ref[0:1, :], mod_ref[1:2, :]).astype(bf16)
    mix = None
    for k, br_ref in enumerate((oa_ref, ob_ref, oc_ref, od_ref)):
        g = _dot(h, wg_ref[:, k * D_MODEL:(k + 1) * D_MODEL])
        y = _dot(br_ref[...], wbr_ref[k])
        term = _sigmoid(g) * y
        mix = term if mix is None else mix + term
    o_ref[...] = x + mod_ref[2:3, :] * _dot(mix.astype(bf16), wout_ref[...])


def _merge_call(xs, mod, branches, w, tm):
    b, t, d = xs.shape
    tiles = t // tm
    x2 = xs.reshape(b * t, d)
    br = [o.reshape(b * t, BRANCH_WIDTH) for o in branches]
    br_spec = pl.BlockSpec((tm, BRANCH_WIDTH), lambda i: (i, 0))
    out = pl.pallas_call(
        _merge_kernel,
        name="merge",
        grid=(b * tiles,),
        in_specs=[
            pl.BlockSpec((tm, d), lambda i: (i, 0)),
            pl.BlockSpec((None, N_MOD, d), lambda i: (i // tiles, 0, 0)),
            br_spec, br_spec, br_spec, br_spec,
            _layer_spec(w["wg"], w["l"]), _layer_spec(w["wbr"], w["l"]), _layer_spec(w["wout"], w["l"]),
        ],
        out_specs=pl.BlockSpec((tm, d), lambda i: (i, 0)),
        out_shape=jax.ShapeDtypeStruct((b * t, d), f32),
        compiler_params=pltpu.CompilerParams(vmem_limit_bytes=VMEM_LIMIT),
    )(x2, mod, *br, w["wg"], w["wbr"], w["wout"])
    return out.reshape(b, t, d)


def _ffn_kernel(x_ref, xp_ref, xn_ref, mod_ref, wup_ref, cw_ref, cb_ref, wdn_ref, fin_ref, o_ref, u_ref, *,
                seq_len, final_norm):
    tm = x_ref.shape[0]
    x = x_ref[...]
    shift, scale = mod_ref[3:4, :], mod_ref[4:5, :]
    tile = pl.program_id(0) % (seq_len // tm)
    keep_prev = (tile != 0).astype(f32)
    keep_next = (tile != seq_len // tm - 1).astype(f32)
    h = jnp.concatenate([
        (_modulate(xp_ref[...], shift, scale) * keep_prev).astype(bf16),
        _modulate(x, shift, scale).astype(bf16),
        (_modulate(xn_ref[...], shift, scale) * keep_next).astype(bf16)], axis=0)

    def conv(c0, width):
        cols = slice(c0, c0 + width)
        u_ref[:, cols] = _dot(h, wup_ref[:, cols])
        prev = u_ref[HALO - 1:HALO - 1 + tm, cols]
        cur = u_ref[HALO:HALO + tm, cols]
        nxt = u_ref[HALO + 1:HALO + 1 + tm, cols]
        return prev * cw_ref[0:1, cols] + cur * cw_ref[1:2, cols] + nxt * cw_ref[2:3, cols] + cb_ref[:, cols]

    acc = None
    cv = 0
    for width in FFN_CHUNKS:
        val = conv(cv, width)
        gate = conv(D_FF + cv, width)
        a = (gate * _sigmoid(gate) * val).astype(bf16)
        part = _dot(a, wdn_ref[cv:cv + width, :])
        acc = part if acc is None else acc + part
        cv += width
    out = x + mod_ref[5:6, :] * acc
    if final_norm:
        out = _rms(out) * fin_ref[...]
    o_ref[...] = out


def _ffn_call(xs, mod, w, fin, tm, final_norm):
    b, t, d = xs.shape
    tiles = t // tm
    x2 = xs.reshape(b * t, d)
    hb = tm // HALO
    last_halo = b * t // HALO - 1
    out = pl.pallas_call(
        functools.partial(_ffn_kernel, seq_len=t, final_norm=final_norm),
        name="conv_ffn",
        grid=(b * tiles,),
        in_specs=[
            pl.BlockSpec((tm, d), lambda i: (i, 0)),
            pl.BlockSpec((HALO, d), lambda i: (jnp.maximum(i * hb - 1, 0), 0)),
            pl.BlockSpec((HALO, d), lambda i: (jnp.minimum((i + 1) * hb, last_halo), 0)),
            pl.BlockSpec((None, N_MOD, d), lambda i: (i // tiles, 0, 0)),
            _layer_spec(w["wup"], w["l"]), _layer_spec(w["cw"], w["l"]), _layer_spec(w["cb"], w["l"]),
            _layer_spec(w["wdn"], w["l"]), _const_spec((1, d)),
        ],
        out_specs=pl.BlockSpec((tm, d), lambda i: (i, 0)),
        out_shape=jax.ShapeDtypeStruct((b * t, d), f32),
        scratch_shapes=[pltpu.VMEM((tm + 2 * HALO, 2 * D_FF), f32)],
        compiler_params=pltpu.CompilerParams(vmem_limit_bytes=VMEM_LIMIT),
    )(x2, x2, x2, mod, w["wup"], w["cw"], w["cb"], w["wdn"], fin)
    return out.reshape(b, t, d)


def _layer_weights(l, stacked, mla_q_norm, mla_kv_norm, mla_w_uq, mla_w_ukv, gqa_q_norm, gqa_k_norm):
    hq = MLA_NOPE_DIM + MLA_ROPE_DIM
    wuq = mla_w_uq[l].reshape(MLA_Q_LORA, MLA_HEADS, hq)
    wuq = jnp.pad(wuq, ((0, 0), (0, 0), (0, LANES - hq))).reshape(MLA_Q_LORA, MLA_HEADS * LANES).astype(bf16)
    wukv = mla_w_ukv[l].reshape(MLA_KV_LORA, MLA_HEADS, MLA_NOPE_DIM + MLA_V_DIM)
    wk = jnp.pad(wukv[:, :, :MLA_NOPE_DIM], ((0, 0), (0, 0), (0, LANES - MLA_NOPE_DIM)))
    wk = wk.reshape(MLA_KV_LORA, MLA_HEADS * LANES).astype(bf16)
    wv = wukv[:, :, MLA_NOPE_DIM:].reshape(MLA_KV_LORA, MLA_HEADS * MLA_V_DIM).astype(bf16)
    return dict(
        stacked, l=l, wuq=wuq, wk=wk, wv=wv,
        qn=mla_q_norm[l].reshape(1, -1), kvn=mla_kv_norm[l].reshape(1, -1),
        gq=jnp.tile(gqa_q_norm[l], 2).reshape(1, LANES), gk=jnp.tile(gqa_k_norm[l], 2).reshape(1, LANES))


def _rope_tables(n, l_ctx):
    t = jnp.arange(n, dtype=jnp.int32)
    row, col = (t // GRID_W).astype(f32), (t % GRID_W).astype(f32)

    def expanded(rot_dim):
        axis_dim = rot_dim // 2
        inv = ROPE_THETA ** (-jnp.arange(0, axis_dim, 2, dtype=f32) / axis_dim)
        ang = jnp.concatenate([row[:, None] * inv, col[:, None] * inv], axis=-1)
        cos = jnp.repeat(jnp.cos(ang), 2, axis=-1)
        sin = jnp.repeat(jnp.sin(ang), 2, axis=-1) * jnp.tile(jnp.array([-1.0, 1.0], f32), rot_dim // 2)
        return cos, sin

    c64, s64 = expanded(2 * HALF // 2)
    c64, s64 = jnp.tile(c64, (1, 2)), jnp.tile(s64, (1, 2))
    c32, s32 = expanded(MLA_ROPE_DIM)
    pad = LANES - MLA_NOPE_DIM - MLA_ROPE_DIM
    cb = jnp.concatenate([jnp.ones((n, MLA_NOPE_DIM), f32), c32, jnp.ones((n, pad), f32)], axis=1)
    sb = jnp.concatenate([jnp.zeros((n, MLA_NOPE_DIM), f32), s32, jnp.zeros((n, pad), f32)], axis=1)
    ones, zeros = jnp.ones((l_ctx, LANES), f32), jnp.zeros((l_ctx, LANES), f32)
    return (c64, s64, cb, sb), (ones, zeros, ones, zeros)


def kernel(x, c, ctx, c_ctx, w_mod, b_mod, w_in, diff_lambda, diff_subln, mla_q_norm, mla_kv_norm, mla_w_uq,
           mla_w_ukv, gqa_q_norm, gqa_k_norm, swa_sink, w_branch, w_out, ffn_w_up, ffn_conv_w, ffn_conv_b,
           ffn_w_down, final_norm):
    b, n, d = x.shape
    l_ctx = ctx.shape[1]
    depth = w_mod.shape[0]
    tm_x = min(512, n)
    tq = min(512, n)
    tk = min(2048, n)

    cs = jnp.zeros((8, d), f32).at[:b].set(c).at[b].set(c_ctx)
    mods = _mod_call(cs, w_mod, b_mod).reshape(depth, 8, N_MOD, d)
    tab_x, tab_c = _rope_tables(n, l_ctx)
    fin = final_norm.reshape(1, d)

    w1, wg = _repack_call(w_in)
    stacked = dict(w1=w1, wg=wg, wbr=w_branch.astype(bf16), wout=w_out.astype(bf16),
                   wup=ffn_w_up.astype(bf16), wdn=ffn_w_down.astype(bf16), cw=ffn_conv_w,
                   cb=ffn_conv_b.reshape(depth, 1, -1))

    xs, xc = x, ctx
    for l in range(depth):
        w = _layer_weights(l, stacked, mla_q_norm, mla_kv_norm, mla_w_uq, mla_w_ukv, gqa_q_norm, gqa_k_norm)
        mod_x = mods[l, :b]
        mod_c = jnp.broadcast_to(mods[l, b][None], (b, N_MOD, d))
        lam_init = 0.8 - 0.6 * math.exp(-0.3 * l)
        a_extras = (diff_lambda[l], diff_subln[l].reshape(1, LANES))
        sink = swa_sink[l]
        update_ctx = l < depth - 1

        px = _proj_call(xs, mod_x, tab_x, w, tm_x)
        pc = _proj_call(xc, mod_c, tab_c, w, l_ctx)
        qa, ka, va, qb, kb, vb, qc, kc, vc, qd, kd, vd = px
        qa_c, ka_c, va_c, qb_c, kb_c, vb_c, qc_c, kc_c, vc_c, qd_c, kd_c, vd_c = pc

        oa = _flash_call("A", qa, ka_c, va_c, ka, va, tq=tq, tk=tk, extras=a_extras, lam_init=lam_init)
        ob = _flash_call("B", qb, kb_c, vb_c, kb, vb, tq=tq, tk=tk)
        oc = _flash_call("C", qc, kc_c, vc_c, kc, vc, tq=tq, tk=tk)
        od = _window_call(qd, kd_c, vd_c, kd, vd, sink, tq=min(256, n))
        xs_mid = _merge_call(xs, mod_x, (oa, ob, oc, od), w, tm_x)
        xs_new = _ffn_call(xs_mid, mod_x, w, fin, tm_x, final_norm=not update_ctx)

        if update_ctx:
            oa = _flash_call("A", qa_c, ka_c, va_c, None, None, tq=l_ctx, tk=tk, extras=a_extras,
                             lam_init=lam_init)
            ob = _flash_call("B", qb_c, kb_c, vb_c, None, None, tq=l_ctx, tk=tk)
            oc = _flash_call("C", qc_c, kc_c, vc_c, None, None, tq=l_ctx, tk=tk)
            od = _flash_call("C", qd_c, kd_c, vd_c, None, None, tq=l_ctx, tk=tk, sink=sink)
            xc_mid = _merge_call(xc, mod_c, (oa, ob, oc, od), w, l_ctx)
            xc = _ffn_call(xc_mid, mod_c, w, fin, l_ctx, final_norm=False)
        xs = xs_new
    return xs
```

```python
import functools
import math

import jax
import jax.numpy as jnp
from jax import lax
from jax.experimental import pallas as pl
from jax.experimental.pallas import tpu as pltpu

f32 = jnp.float32
bf16 = jnp.bfloat16

LANES = 128
HALF = 64
D_MODEL = 1024
GRID_W = 64
WINDOW = 128
N_BRANCH = 4
N_MOD = 6
EPS = 1e-6
ROPE_THETA = 10000.0
NEG_INF = -1e30
LOG2E = math.log2(math.e)
DIFF_HEADS = 4
DIFF_QK_DIM = 64
MLA_HEADS = 8
MLA_Q_LORA = 256
MLA_KV_LORA = 256
MLA_NOPE_DIM = 64
MLA_ROPE_DIM = 32
MLA_V_DIM = 64
GQA_HEADS = 8
BRANCH_WIDTH = 512
D_FF = 2816
FFN_CHUNKS = (2816,)
assert sum(FFN_CHUNKS) == D_FF
HALO = 8
VMEM_LIMIT = 56 * 1024 * 1024

IN_WIDTH = 7712
ALIGNED_COLS = 2048
ROT_COLS = 32
W1_WIDTH = 4224
GATE_WIDTH = N_BRANCH * D_MODEL


def _dot(a, b):
    return jnp.dot(a, b, preferred_element_type=f32)


def _dot_nt(a, b):
    return lax.dot_general(a, b, (((1,), (1,)), ((), ())), preferred_element_type=f32)


def _sigmoid(v):
    return 1.0 / (1.0 + jnp.exp(-v))


def _rms(v):
    return v * lax.rsqrt(jnp.mean(v * v, axis=-1, keepdims=True) + EPS)


def _modulate(v, shift, scale):
    return _rms(v) * (1.0 + scale) + shift


def _const_spec(shape):
    zeros = (0,) * len(shape)
    return pl.BlockSpec(shape, lambda *_: zeros, pipeline_mode=pl.Buffered(1))


def _layer_spec(stacked, l):
    tail = stacked.shape[1:]
    idx = (l,) + (0,) * len(tail)
    return pl.BlockSpec((None,) + tail, lambda *_: idx, pipeline_mode=pl.Buffered(1))


def _mod_kernel(c_ref, w_ref, b_ref, o_ref):
    cv = c_ref[...]
    a = cv * _sigmoid(cv)
    o_ref[...] = _dot(a, w_ref[...]) + b_ref[...]


def _mod_call(cs, w_mod, b_mod):
    depth, d, width = w_mod.shape
    tn = 1536
    return pl.pallas_call(
        _mod_kernel,
        name="mod_vectors",
        grid=(depth, width // tn),
        in_specs=[
            pl.BlockSpec((8, d), lambda l, j: (0, 0)),
            pl.BlockSpec((None, d, tn), lambda l, j: (l, 0, j)),
            pl.BlockSpec((None, 1, tn), lambda l, j: (l, 0, j)),
        ],
        out_specs=pl.BlockSpec((None, 8, tn), lambda l, j: (l, 0, j)),
        out_shape=jax.ShapeDtypeStruct((depth, 8, width), f32),
        compiler_params=pltpu.CompilerParams(vmem_limit_bytes=VMEM_LIMIT),
    )(cs, w_mod, b_mod.reshape(depth, 1, width))


def _repack_kernel(wt_ref, w1_ref, wg_ref):
    tr = wt_ref.shape[1]

    def rows(r0, n):
        return wt_ref[r0:r0 + n, :]

    def emit(dst_ref, out, block):
        dst_ref[:, out:out + LANES] = block.T.astype(bf16)

    for c in range(ALIGNED_COLS // LANES):
        emit(w1_ref, c * LANES, rows(c * LANES, LANES))
    pad = LANES - MLA_NOPE_DIM - ROT_COLS
    emit(w1_ref, ALIGNED_COLS, jnp.concatenate(
        [jnp.zeros((MLA_NOPE_DIM, tr), f32), rows(ALIGNED_COLS, ROT_COLS), jnp.zeros((pad, tr), f32)], axis=0))

    out = ALIGNED_COLS + LANES
    src = ALIGNED_COLS + ROT_COLS
    for _ in range(2):
        for _ in range(4):
            emit(w1_ref, out, rows(src, LANES))
            out, src = out + LANES, src + LANES
        for _ in range(2):
            for kv in range(2):
                half = rows(src + kv * HALF, HALF)
                emit(w1_ref, out, jnp.concatenate([half, half], axis=0))
                out += LANES
            src += LANES
    assert out == W1_WIDTH and IN_WIDTH - src == GATE_WIDTH
    for j in range(GATE_WIDTH // LANES):
        emit(wg_ref, j * LANES, rows(src + j * LANES, LANES))


def _repack_call(w_in):
    depth, d, width = w_in.shape
    assert width == IN_WIDTH
    tr = 256
    w_t = jnp.swapaxes(w_in, 1, 2)
    return pl.pallas_call(
        _repack_kernel,
        name="repack_w_in",
        grid=(depth, d // tr),
        in_specs=[pl.BlockSpec((None, width, tr), lambda l, i: (l, 0, i))],
        out_specs=[pl.BlockSpec((None, tr, W1_WIDTH), lambda l, i: (l, i, 0)),
                   pl.BlockSpec((None, tr, GATE_WIDTH), lambda l, i: (l, i, 0))],
        out_shape=[jax.ShapeDtypeStruct((depth, d, W1_WIDTH), bf16),
                   jax.ShapeDtypeStruct((depth, d, GATE_WIDTH), bf16)],
        compiler_params=pltpu.CompilerParams(vmem_limit_bytes=VMEM_LIMIT),
    )(w_t)


def _proj_kernel(x_ref, mod_ref, c64_ref, s64_ref, cb_ref, sb_ref, w1_ref, qn_ref, kvn_ref, wuq_ref, wk_ref,
                 wv_ref, gq_ref, gk_ref,
                 qa_ref, ka_ref, va_ref, qb_ref, kb_ref, vb_ref, qc_ref, kc_ref, vc_ref, qd_ref, kd_ref, vd_ref,
                 *, scale_b):
    tm = x_ref.shape[0]
    h = _modulate(x_ref[...], mod_ref[0:1, :], mod_ref[1:2, :]).astype(bf16)
    lane = lax.broadcasted_iota(jnp.int32, (tm, LANES), 1)
    even = (lane & 1) == 0
    lo = lane < HALF
    c64, s64, cb, sb = c64_ref[...], s64_ref[...], cb_ref[...], sb_ref[...]

    def rope(v, cos, sin):
        partner = jnp.where(even, pltpu.roll(v, LANES - 1, 1), pltpu.roll(v, 1, 1))
        return v * cos + partner * sin

    def head_norm(v, gain):
        sq = v * v
        s_lo = jnp.sum(jnp.where(lo, sq, 0.0), axis=-1, keepdims=True)
        s_hi = jnp.sum(jnp.where(lo, 0.0, sq), axis=-1, keepdims=True)
        inv = jnp.where(lo, lax.rsqrt(s_lo * (1.0 / HALF) + EPS), lax.rsqrt(s_hi * (1.0 / HALF) + EPS))
        return v * inv * gain

    def chunk(z, j):
        return z[:, j * LANES:(j + 1) * LANES]

    qk_scale = DIFF_QK_DIM ** -0.5 * LOG2E

    z = _dot(h, w1_ref[:, 0:1536])
    for j in range(4):
        qa_ref[j] = (rope(chunk(z, j), c64, s64) * qk_scale).astype(bf16)
        ka_ref[j] = rope(chunk(z, 4 + j), c64, s64).astype(bf16)
        va_ref[j] = chunk(z, 8 + j).astype(bf16)

    z = _dot(h, w1_ref[:, 1536:2176])
    dq = (_rms(z[:, 0:256]) * qn_ref[...]).astype(bf16)
    ckv = (_rms(z[:, 256:512]) * kvn_ref[...]).astype(bf16)
    k_rot = rope(z[:, 512:640], cb, sb)
    zq = _dot(dq, wuq_ref[...])
    zk = _dot(ckv, wk_ref[...])
    zv = _dot(ckv, wv_ref[...])
    for j in range(MLA_HEADS):
        qb_ref[j] = (rope(chunk(zq, j), cb, sb) * scale_b).astype(bf16)
        kb_ref[j] = (chunk(zk, j) + k_rot).astype(bf16)
    for j in range(4):
        vb_ref[j] = chunk(zv, j).astype(bf16)

    z = _dot(h, w1_ref[:, 2176:3200])
    gq, gk = gq_ref[...], gk_ref[...]
    for j in range(4):
        qc_ref[j] = (rope(head_norm(chunk(z, j), gq), c64, s64) * qk_scale).astype(bf16)
    for j in range(2):
        kc_ref[j] = rope(head_norm(chunk(z, 4 + j), gk), c64, s64).astype(bf16)
        vc_ref[j] = chunk(z, 6 + j).astype(bf16)

    z = _dot(h, w1_ref[:, 3200:4224])
    for j in range(4):
        qd_ref[j] = (rope(chunk(z, j), c64, s64) * qk_scale).astype(bf16)
    for j in range(2):
        kd_ref[j] = rope(chunk(z, 4 + j), c64, s64).astype(bf16)
        vd_ref[j] = chunk(z, 6 + j).astype(bf16)


def _proj_call(xs, mod, tables, w, tm):
    b, t, d = xs.shape
    c64, s64, cb, sb = tables
    grid = (b, t // tm)

    def out(nchunks):
        return (jax.ShapeDtypeStruct((b, nchunks, t, LANES), bf16),
                pl.BlockSpec((None, nchunks, tm, LANES), lambda bi, i: (bi, 0, i, 0)))

    outs = [out(n) for n in (4, 4, 4, 8, 8, 4, 4, 2, 2, 4, 2, 2)]
    tab_spec = pl.BlockSpec((tm, LANES), lambda bi, i: (i, 0))
    scale_b = (MLA_NOPE_DIM + MLA_ROPE_DIM) ** -0.5 * LOG2E
    return pl.pallas_call(
        functools.partial(_proj_kernel, scale_b=scale_b),
        name="in_proj",
        grid=grid,
        in_specs=[
            pl.BlockSpec((None, tm, d), lambda bi, i: (bi, i, 0)),
            pl.BlockSpec((None, N_MOD, d), lambda bi, i: (bi, 0, 0)),
            tab_spec, tab_spec, tab_spec, tab_spec,
            _layer_spec(w["w1"], w["l"]), _const_spec((1, 256)), _const_spec((1, 256)),
            _const_spec(w["wuq"].shape), _const_spec(w["wk"].shape), _const_spec(w["wv"].shape),
            _const_spec((1, LANES)), _const_spec((1, LANES)),
        ],
        out_specs=[o[1] for o in outs],
        out_shape=[o[0] for o in outs],
        compiler_params=pltpu.CompilerParams(vmem_limit_bytes=VMEM_LIMIT),
    )(xs, mod, c64, s64, cb, sb, w["w1"], w["qn"], w["kvn"], w["wuq"], w["wk"], w["wv"], w["gq"], w["gk"])


def _flash_kernel(*refs, mode, has_x, has_sink, tk, lam_init):
    refs = list(refs)
    q_ref, kc_ref, vc_ref = refs[:3]
    pos = 3
    if has_x:
        k_ref, v_ref = refs[pos:pos + 2]
        pos += 2
    if mode == "A":
        dl_ref, subln_ref = refs[pos:pos + 2]
        pos += 2
    if has_sink:
        sink_ref = refs[pos]
        pos += 1
    o_ref, q2_ref, m_ref, acc_ref = refs[pos:pos + 4]
    s_refs, p_refs, al_refs = refs[pos + 4:pos + 6], refs[pos + 6:pos + 8], refs[pos + 8:pos + 10]

    tq = q_ref.shape[1]
    lane = lax.broadcasted_iota(jnp.int32, (tq, LANES), 1)
    lo = lane < HALF
    shared_k = mode != "B"
    if shared_k:
        q = q_ref[0]
        zero = jnp.zeros_like(q)
        q2_ref[0:tq, :] = jnp.where(lo, q, zero)
        q2_ref[tq:2 * tq, :] = jnp.where(lo, zero, q)
    else:
        q2_ref[0:tq, :] = q_ref[0]
        q2_ref[tq:2 * tq, :] = q_ref[1]

    for u in range(2):
        if has_sink:
            sink = sink_ref[2 * pl.program_id(1) + u] * LOG2E
            m_ref[u] = jnp.full((tq, LANES), sink, f32)
            acc_ref[u, :, LANES:] = jnp.ones((tq, LANES), f32)
        else:
            m_ref[u] = jnp.full((tq, LANES), NEG_INF, f32)
            acc_ref[u, :, LANES:] = jnp.zeros((tq, LANES), f32)
        acc_ref[u, :, :LANES] = jnp.zeros((tq, LANES), f32)

    def scores(chunk, s_ref):
        kref, _, start, size = chunk
        if shared_k:
            s_ref[:, :size] = _dot_nt(q2_ref[...], kref[0, start:start + size, :])
        else:
            for u in range(2):
                rows = slice(u * tq, (u + 1) * tq)
                s_ref[rows, :size] = _dot_nt(q2_ref[rows, :], kref[u, start:start + size, :])

    def softmax(chunk, s_ref, p_ref, al_ref):
        size = chunk[3]
        for u in range(2):
            rows = slice(u * tq, (u + 1) * tq)
            cols = [s_ref[rows, j * LANES:(j + 1) * LANES] for j in range(size // LANES)]
            m_prev = m_ref[u]
            m_new = jnp.maximum(m_prev, jnp.max(functools.reduce(jnp.maximum, cols), axis=-1, keepdims=True))
            al_ref[u] = jnp.exp2(m_prev - m_new)
            for j, cj in enumerate(cols):
                p_ref[rows, j * LANES:(j + 1) * LANES] = jnp.exp2(cj - m_new).astype(bf16)
            m_ref[u] = m_new

    def pv(chunk, p_ref, al_ref):
        _, vref, start, size = chunk
        vext = jnp.concatenate([vref[0, start:start + size, :], jnp.ones((size, LANES), bf16)], axis=1)
        for u in range(2):
            alpha = al_ref[u]
            alpha2 = jnp.concatenate([alpha, alpha], axis=1)
            acc_ref[u] = alpha2 * acc_ref[u] + _dot(p_ref[u * tq:(u + 1) * tq, :size], vext)

    chunks = [(kc_ref, vc_ref, 0, kc_ref.shape[1])]
    if has_x:
        chunks += [(k_ref, v_ref, c * tk, tk) for c in range(k_ref.shape[1] // tk)]

    for i in range(len(chunks) + 2):
        if i < len(chunks):
            scores(chunks[i], s_refs[i % 2])
        if 1 <= i <= len(chunks):
            softmax(chunks[i - 1], s_refs[(i - 1) % 2], p_refs[(i - 1) % 2], al_refs[(i - 1) % 2])
        if i >= 2:
            pv(chunks[i - 2], p_refs[i % 2], al_refs[i % 2])

    o0 = acc_ref[0, :, :LANES] / acc_ref[0, :, LANES:]
    o1 = acc_ref[1, :, :LANES] / acc_ref[1, :, LANES:]
    if mode == "A":
        dl = dl_ref[...]
        lam = (jnp.exp(jnp.sum(dl[0:1] * dl[1:2], axis=-1, keepdims=True))
               - jnp.exp(jnp.sum(dl[2:3] * dl[3:4], axis=-1, keepdims=True)) + lam_init)
        o = _rms(o0 - lam * o1) * subln_ref[...] * (1.0 - lam_init)
    else:
        o = jnp.where(lo, o0, o1)
    o_ref[...] = o.astype(bf16)


def _flash_call(mode, q, kc, vc, k, v, *, tq, tk, extras=(), sink=None, lam_init=0.0):
    b, _, t, _ = q.shape
    l_ctx = kc.shape[2]
    has_x = k is not None
    gq = 2 if mode == "B" else 1
    shared_kv = mode == "C"

    def kv_idx(bi, g, i):
        return (bi, g // 2 if shared_kv else g, 0, 0)

    in_specs = [
        pl.BlockSpec((None, gq, tq, LANES), lambda bi, g, i: (bi, g, i, 0)),
        pl.BlockSpec((None, gq, l_ctx, LANES), kv_idx),
        pl.BlockSpec((None, 1, l_ctx, LANES), kv_idx),
    ]
    args = [q, kc, vc]
    if has_x:
        n = k.shape[2]
        in_specs += [pl.BlockSpec((None, gq, n, LANES), kv_idx), pl.BlockSpec((None, 1, n, LANES), kv_idx)]
        args += [k, v]
    for e in extras:
        in_specs.append(pl.BlockSpec(e.shape, lambda bi, g, i, nd=e.ndim: (0,) * nd))
        args.append(e)
    if sink is not None:
        in_specs.append(pl.BlockSpec(memory_space=pltpu.SMEM))
        args.append(sink)
    state = pltpu.VMEM((2, tq, LANES), f32)
    width = max(tk, l_ctx) if has_x else l_ctx
    if has_x:
        assert k.shape[2] % tk == 0 and tk % LANES == 0
    s_buf, p_buf = pltpu.VMEM((2 * tq, width), f32), pltpu.VMEM((2 * tq, width), bf16)
    scratch = [pltpu.VMEM((2 * tq, LANES), bf16), state, pltpu.VMEM((2, tq, 2 * LANES), f32),
               s_buf, s_buf, p_buf, p_buf, state, state]
    return pl.pallas_call(
        functools.partial(_flash_kernel, mode=mode, has_x=has_x, has_sink=sink is not None, tk=tk,
                          lam_init=lam_init),
        name="flash_" + mode + ("" if has_x else "_ctx"),
        grid=(b, 4, t // tq),
        in_specs=in_specs,
        out_specs=pl.BlockSpec((None, tq, LANES), lambda bi, g, i: (bi, i, g)),
        out_shape=jax.ShapeDtypeStruct((b, t, 4 * LANES), bf16),
        scratch_shapes=scratch,
        compiler_params=pltpu.CompilerParams(vmem_limit_bytes=VMEM_LIMIT),
    )(*args)


def _window_kernel(q_ref, kc_ref, vc_ref, k_ref, v_ref, sink_ref, o_ref):
    tq = q_ref.shape[1]
    n = k_ref.shape[1]
    l_ctx = kc_ref.shape[1]
    band = tq + 2 * WINDOW
    width = l_ctx + band
    t0 = pl.program_id(1) * tq
    start = pl.multiple_of(jnp.clip(t0 - WINDOW, 0, n - band), WINDOW)
    col = lax.broadcasted_iota(jnp.int32, (tq, width), 1)
    qpos = t0 + lax.broadcasted_iota(jnp.int32, (tq, width), 0)
    valid = (col < l_ctx) | (jnp.abs(col - l_ctx + start - qpos) <= WINDOW)
    lane = lax.broadcasted_iota(jnp.int32, (tq, LANES), 1)
    lo = lane < HALF
    for j in range(4):
        if j % 2 == 0:
            kv = j // 2
            k_all = jnp.concatenate([kc_ref[kv], k_ref[kv, pl.ds(start, band), :]], axis=0)
            v_all = jnp.concatenate([vc_ref[kv], v_ref[kv, pl.ds(start, band), :]], axis=0)
            v_ext = jnp.concatenate([v_all, jnp.ones_like(v_all)], axis=1)
        q = q_ref[j]
        zero = jnp.zeros_like(q)
        s = _dot_nt(jnp.concatenate([jnp.where(lo, q, zero), jnp.where(lo, zero, q)], axis=0), k_all)
        outs = []
        for u in range(2):
            sink = sink_ref[2 * j + u] * LOG2E
            su = jnp.where(valid, s[u * tq:(u + 1) * tq], NEG_INF)
            cols = [su[:, c * LANES:(c + 1) * LANES] for c in range(width // LANES)]
            m = jnp.maximum(jnp.max(functools.reduce(jnp.maximum, cols), axis=-1, keepdims=True), sink)
            p = jnp.concatenate([jnp.exp2(c - m).astype(bf16) for c in cols], axis=1)
            acc = _dot(p, v_ext)
            outs.append(acc[:, :LANES] / (acc[:, LANES:] + jnp.exp2(sink - m)))
        o_ref[:, j * LANES:(j + 1) * LANES] = jnp.where(lo, outs[0], outs[1]).astype(bf16)


def _window_call(q, kc, vc, k, v, sink, *, tq):
    b, _, n, _ = q.shape
    l_ctx = kc.shape[2]
    return pl.pallas_call(
        _window_kernel,
        name="window_attn",
        grid=(b, n // tq),
        in_specs=[
            pl.BlockSpec((None, 4, tq, LANES), lambda bi, i: (bi, 0, i, 0)),
            pl.BlockSpec((None, 2, l_ctx, LANES), lambda bi, i: (bi, 0, 0, 0)),
            pl.BlockSpec((None, 2, l_ctx, LANES), lambda bi, i: (bi, 0, 0, 0)),
            pl.BlockSpec((None, 2, n, LANES), lambda bi, i: (bi, 0, 0, 0)),
            pl.BlockSpec((None, 2, n, LANES), lambda bi, i: (bi, 0, 0, 0)),
            pl.BlockSpec(memory_space=pltpu.SMEM),
        ],
        out_specs=pl.BlockSpec((None, tq, 4 * LANES), lambda bi, i: (bi, i, 0)),
        out_shape=jax.ShapeDtypeStruct((b, n, 4 * LANES), bf16),
        compiler_params=pltpu.CompilerParams(vmem_limit_bytes=VMEM_LIMIT),
    )(q, kc, vc, k, v, sink)


def _merge_kernel(x_ref, mod_ref, oa_ref, ob_ref, oc_ref, od_ref, wg_ref, wbr_ref, wout_ref, o_ref):
    x = x_ref[...]
    h = _modulate(x, mod_ref[0:1, :], mod_ref[1:2, :]).astype(bf16)
    mix = None
    for k, br_ref in enumerate((oa_ref, ob_ref, oc_ref, od_ref)):
        g = _dot(h, wg_ref[:, k * D_MODEL:(k + 1) * D_MODEL])
        y = _dot(br_ref[...], wbr_ref[k])
        term = _sigmoid(g) * y
        mix = term if mix is None else mix + term
    o_ref[...] = x + mod_ref[2:3, :] * _dot(mix.astype(bf16), wout_ref[...])


def _merge_call(xs, mod, branches, w, tm):
    b, t, d = xs.shape
    tiles = t // tm
    x2 = xs.reshape(b * t, d)
    br = [o.reshape(b * t, BRANCH_WIDTH) for o in branches]
    br_spec = pl.BlockSpec((tm, BRANCH_WIDTH), lambda i: (i, 0))
    out = pl.pallas_call(
        _merge_kernel,
        name="merge",
        grid=(b * tiles,),
        in_specs=[
            pl.BlockSpec((tm, d), lambda i: (i, 0)),
            pl.BlockSpec((None, N_MOD, d), lambda i: (i // tiles, 0, 0)),
            br_spec, br_spec, br_spec, br_spec,
            _layer_spec(w["wg"], w["l"]), _layer_spec(w["wbr"], w["l"]), _layer_spec(w["wout"], w["l"]),
        ],
        out_specs=pl.BlockSpec((tm, d), lambda i: (i, 0)),
        out_shape=jax.ShapeDtypeStruct((b * t, d), f32),
        compiler_params=pltpu.CompilerParams(vmem_limit_bytes=VMEM_LIMIT),
    )(x2, mod, *br, w["wg"], w["wbr"], w["wout"])
    return out.reshape(b, t, d)


def _ffn_kernel(x_ref, xp_ref, xn_ref, mod_ref, wup_ref, cw_ref, cb_ref, wdn_ref, fin_ref, o_ref, u_ref, *,
                seq_len, final_norm):
    tm = x_ref.shape[0]
    x = x_ref[...]
    shift, scale = mod_ref[3:4, :], mod_ref[4:5, :]
    tile = pl.program_id(0) % (seq_len // tm)
    keep_prev = (tile != 0).astype(f32)
    keep_next = (tile != seq_len // tm - 1).astype(f32)
    h = jnp.concatenate([
        (_modulate(xp_ref[...], shift, scale) * keep_prev).astype(bf16),
        _modulate(x, shift, scale).astype(bf16),
        (_modulate(xn_ref[...], shift, scale) * keep_next).astype(bf16)], axis=0)

    def conv(c0, width):
        cols = slice(c0, c0 + width)
        u_ref[:, cols] = _dot(h, wup_ref[:, cols])
        prev = u_ref[HALO - 1:HALO - 1 + tm, cols]
        cur = u_ref[HALO:HALO + tm, cols]
        nxt = u_ref[HALO + 1:HALO + 1 + tm, cols]
        return prev * cw_ref[0:1, cols] + cur * cw_ref[1:2, cols] + nxt * cw_ref[2:3, cols] + cb_ref[:, cols]

    acc = None
    cv = 0
    for width in FFN_CHUNKS:
        val = conv(cv, width)
        gate = conv(D_FF + cv, width)
        a = (gate * _sigmoid(gate) * val).astype(bf16)
        part = _dot(a, wdn_ref[cv:cv + width, :])
        acc = part if acc is None else acc + part
        cv += width
    out = x + mod_ref[5:6, :] * acc
    if final_norm:
        out = _rms(out) * fin_ref[...]
    o_ref[...] = out


def _ffn_call(xs, mod, w, fin, tm, final_norm):
    b, t, d = xs.shape
    tiles = t // tm
    x2 = xs.reshape(b * t, d)
    hb = tm // HALO
    last_halo = b * t // HALO - 1
    out = pl.pallas_call(
        functools.partial(_ffn_kernel, seq_len=t, final_norm=final_norm),
        name="conv_ffn",
        grid=(b * tiles,),
        in_specs=[
            pl.BlockSpec((tm, d), lambda i: (i, 0)),
            pl.BlockSpec((HALO, d), lambda i: (jnp.maximum(i * hb - 1, 0), 0)),
            pl.BlockSpec((HALO, d), lambda i: (jnp.minimum((i + 1) * hb, last_halo), 0)),
            pl.BlockSpec((None, N_MOD, d), lambda i: (i // tiles, 0, 0)),
            _layer_spec(w["wup"], w["l"]), _layer_spec(w["cw"], w["l"]), _layer_spec(w["cb"], w["l"]),
            _layer_spec(w["wdn"], w["l"]), _const_spec((1, d)),
        ],
        out_specs=pl.BlockSpec((tm, d), lambda i: (i, 0)),
        out_shape=jax.ShapeDtypeStruct((b * t, d), f32),
        scratch_shapes=[pltpu.VMEM((tm + 2 * HALO, 2 * D_FF), f32)],
        compiler_params=pltpu.CompilerParams(vmem_limit_bytes=VMEM_LIMIT),
    )(x2, x2, x2, mod, w["wup"], w["cw"], w["cb"], w["wdn"], fin)
    return out.reshape(b, t, d)


def _layer_weights(l, stacked, mla_q_norm, mla_kv_norm, mla_w_uq, mla_w_ukv, gqa_q_norm, gqa_k_norm):
    hq = MLA_NOPE_DIM + MLA_ROPE_DIM
    wuq = mla_w_uq[l].reshape(MLA_Q_LORA, MLA_HEADS, hq)
    wuq = jnp.pad(wuq, ((0, 0), (0, 0), (0, LANES - hq))).reshape(MLA_Q_LORA, MLA_HEADS * LANES).astype(bf16)
    wukv = mla_w_ukv[l].reshape(MLA_KV_LORA, MLA_HEADS, MLA_NOPE_DIM + MLA_V_DIM)
    wk = jnp.pad(wukv[:, :, :MLA_NOPE_DIM], ((0, 0), (0, 0), (0, LANES - MLA_NOPE_DIM)))
    wk = wk.reshape(MLA_KV_LORA, MLA_HEADS * LANES).astype(bf16)
    wv = wukv[:, :, MLA_NOPE_DIM:].reshape(MLA_KV_LORA, MLA_HEADS * MLA_V_DIM).astype(bf16)
    return dict(
        stacked, l=l, wuq=wuq, wk=wk, wv=wv,
        qn=mla_q_norm[l].reshape(1, -1), kvn=mla_kv_norm[l].reshape(1, -1),
        gq=jnp.tile(gqa_q_norm[l], 2).reshape(1, LANES), gk=jnp.tile(gqa_k_norm[l], 2).reshape(1, LANES))


def _rope_tables(n, l_ctx):
    t = jnp.arange(n, dtype=jnp.int32)
    row, col = (t // GRID_W).astype(f32), (t % GRID_W).astype(f32)

    def expanded(rot_dim):
        axis_dim = rot_dim // 2
        inv = ROPE_THETA ** (-jnp.arange(0, axis_dim, 2, dtype=f32) / axis_dim)
        ang = jnp.concatenate([row[:, None] * inv, col[:, None] * inv], axis=-1)
        cos = jnp.repeat(jnp.cos(ang), 2, axis=-1)
        sin = jnp.repeat(jnp.sin(ang), 2, axis=-1) * jnp.tile(jnp.array([-1.0, 1.0], f32), rot_dim // 2)
        return cos, sin

    c64, s64 = expanded(2 * HALF // 2)
    c64, s64 = jnp.tile(c64, (1, 2)), jnp.tile(s64, (1, 2))
    c32, s32 = expanded(MLA_ROPE_DIM)
    pad = LANES - MLA_NOPE_DIM - MLA_ROPE_DIM
    cb = jnp.concatenate([jnp.ones((n, MLA_NOPE_DIM), f32), c32, jnp.ones((n, pad), f32)], axis=1)
    sb = jnp.concatenate([jnp.zeros((n, MLA_NOPE_DIM), f32), s32, jnp.zeros((n, pad), f32)], axis=1)
    ones, zeros = jnp.ones((l_ctx, LANES), f32), jnp.zeros((l_ctx, LANES), f32)
    return (c64, s64, cb, sb), (ones, zeros, ones, zeros)


def kernel(x, c, ctx, c_ctx, w_mod, b_mod, w_in, diff_lambda, diff_subln, mla_q_norm, mla_kv_norm, mla_w_uq,
           mla_w_ukv, gqa_q_norm, gqa_k_norm, swa_sink, w_branch, w_out, ffn_w_up, ffn_conv_w, ffn_conv_b,
           ffn_w_down, final_norm):
    b, n, d = x.shape
    l_ctx = ctx.shape[1]
    depth = w_mod.shape[0]
    tm_x = min(512, n)
    tq = min(512, n)
    tq_b = min(256, n)
    tk = min(1024, n)

    cs = jnp.zeros((8, d), f32).at[:b].set(c).at[b].set(c_ctx)
    mods = _mod_call(cs, w_mod, b_mod).reshape(depth, 8, N_MOD, d)
    tab_x, tab_c = _rope_tables(n, l_ctx)
    fin = final_norm.reshape(1, d)

    w1, wg = _repack_call(w_in)
    stacked = dict(w1=w1, wg=wg, wbr=w_branch.astype(bf16), wout=w_out.astype(bf16),
                   wup=ffn_w_up.astype(bf16), wdn=ffn_w_down.astype(bf16), cw=ffn_conv_w,
                   cb=ffn_conv_b.reshape(depth, 1, -1))

    xs, xc = x, ctx
    for l in range(depth):
        w = _layer_weights(l, stacked, mla_q_norm, mla_kv_norm, mla_w_uq, mla_w_ukv, gqa_q_norm, gqa_k_norm)
        mod_x = mods[l, :b]
        mod_c = jnp.broadcast_to(mods[l, b][None], (b, N_MOD, d))
        lam_init = 0.8 - 0.6 * math.exp(-0.3 * l)
        a_extras = (diff_lambda[l], diff_subln[l].reshape(1, LANES))
        sink = swa_sink[l]
        update_ctx = l < depth - 1

        px = _proj_call(xs, mod_x, tab_x, w, tm_x)
        pc = _proj_call(xc, mod_c, tab_c, w, l_ctx)
        qa, ka, va, qb, kb, vb, qc, kc, vc, qd, kd, vd = px
        qa_c, ka_c, va_c, qb_c, kb_c, vb_c, qc_c, kc_c, vc_c, qd_c, kd_c, vd_c = pc

        oa = _flash_call("A", qa, ka_c, va_c, ka, va, tq=tq, tk=tk, extras=a_extras, lam_init=lam_init)
        ob = _flash_call("B", qb, kb_c, vb_c, kb, vb, tq=tq_b, tk=tk)
        oc = _flash_call("C", qc, kc_c, vc_c, kc, vc, tq=tq, tk=tk)
        od = _window_call(qd, kd_c, vd_c, kd, vd, sink, tq=min(256, n))
        xs_mid = _merge_call(xs, mod_x, (oa, ob, oc, od), w, tm_x)
        xs_new = _ffn_call(xs_mid, mod_x, w, fin, tm_x, final_norm=not update_ctx)

        if update_ctx:
            oa = _flash_call("A", qa_c, ka_c, va_c, None, None, tq=l_ctx, tk=tk, extras=a_extras,
                             lam_init=lam_init)
            ob = _flash_call("B", qb_c, kb_c, vb_c, None, None, tq=l_ctx, tk=tk)
            oc = _flash_call("C", qc_c, kc_c, vc_c, None, None, tq=l_ctx, tk=tk)
            od = _flash_call("C", qd_c, kd_c, vd_c, None, None, tq=l_ctx, tk=tk, sink=sink)
            xc_mid = _merge_call(xc, mod_c, (oa, ob, oc, od), w, l_ctx)
            xc = _ffn_call(xc_mid, mod_c, w, fin, l_ctx, final_norm=False)
        xs = xs_new
    return xs
```

```python
import functools
import math

import jax
import jax.numpy as jnp
from jax import lax
from jax.experimental import pallas as pl
from jax.experimental.pallas import tpu as pltpu

f32 = jnp.float32
bf16 = jnp.bfloat16

LANES = 128
HALF = 64
D_MODEL = 1024
GRID_W = 64
WINDOW = 128
N_BRANCH = 4
N_MOD = 6
EPS = 1e-6
ROPE_THETA = 10000.0
NEG_INF = -1e30
LOG2E = math.log2(math.e)
DIFF_QK_DIM = 64
MLA_HEADS = 8
MLA_Q_LORA = 256
MLA_KV_LORA = 256
MLA_NOPE_DIM = 64
MLA_ROPE_DIM = 32
MLA_V_DIM = 64
BRANCH_WIDTH = 512
D_FF = 2816
FFN_CHUNKS = (2816,)
assert sum(FFN_CHUNKS) == D_FF
HALO = 8
VMEM_LIMIT = 56 * 1024 * 1024

IN_WIDTH = 7712
ALIGNED_COLS = 2048
ROT_COLS = 32
W1_WIDTH = 4224
GATE_WIDTH = N_BRANCH * D_MODEL


def _dot(a, b):
    return jnp.dot(a, b, preferred_element_type=f32)


def _dot_nt(a, b):
    return lax.dot_general(a, b, (((1,), (1,)), ((), ())), preferred_element_type=f32)


def _sigmoid(v):
    return 1.0 / (1.0 + jnp.exp(-v))


def _rms(v):
    return v * lax.rsqrt(jnp.mean(v * v, axis=-1, keepdims=True) + EPS)


def _modulate(v, shift, scale):
    return _rms(v) * (1.0 + scale) + shift


def _const_spec(shape):
    zeros = (0,) * len(shape)
    return pl.BlockSpec(shape, lambda *_: zeros, pipeline_mode=pl.Buffered(1))


def _layer_spec(stacked, l):
    tail = stacked.shape[1:]
    idx = (l,) + (0,) * len(tail)
    return pl.BlockSpec((None,) + tail, lambda *_: idx, pipeline_mode=pl.Buffered(1))


def _mod_kernel(c_ref, w_ref, b_ref, o_ref):
    cv = c_ref[...]
    a = cv * _sigmoid(cv)
    o_ref[...] = _dot(a, w_ref[...]) + b_ref[...]


def _mod_call(cs, w_mod, b_mod):
    depth, d, width = w_mod.shape
    tn = 1536
    return pl.pallas_call(
        _mod_kernel,
        name="mod_vectors",
        grid=(depth, width // tn),
        in_specs=[
            pl.BlockSpec((8, d), lambda l, j: (0, 0)),
            pl.BlockSpec((None, d, tn), lambda l, j: (l, 0, j)),
            pl.BlockSpec((None, 1, tn), lambda l, j: (l, 0, j)),
        ],
        out_specs=pl.BlockSpec((None, 8, tn), lambda l, j: (l, 0, j)),
        out_shape=jax.ShapeDtypeStruct((depth, 8, width), f32),
        compiler_params=pltpu.CompilerParams(vmem_limit_bytes=VMEM_LIMIT),
    )(cs, w_mod, b_mod.reshape(depth, 1, width))


def _repack_kernel(wt_ref, w1_ref, wg_ref):
    tr = wt_ref.shape[1]

    def rows(r0, n):
        return wt_ref[r0:r0 + n, :]

    def emit(dst_ref, out, block):
        dst_ref[:, out:out + LANES] = block.T.astype(bf16)

    for c in range(ALIGNED_COLS // LANES):
        emit(w1_ref, c * LANES, rows(c * LANES, LANES))
    pad = LANES - MLA_NOPE_DIM - ROT_COLS
    emit(w1_ref, ALIGNED_COLS, jnp.concatenate(
        [jnp.zeros((MLA_NOPE_DIM, tr), f32), rows(ALIGNED_COLS, ROT_COLS), jnp.zeros((pad, tr), f32)], axis=0))

    out = ALIGNED_COLS + LANES
    src = ALIGNED_COLS + ROT_COLS
    for _ in range(2):
        for _ in range(4):
            emit(w1_ref, out, rows(src, LANES))
            out, src = out + LANES, src + LANES
        for _ in range(2):
            for kv in range(2):
                half = rows(src + kv * HALF, HALF)
                emit(w1_ref, out, jnp.concatenate([half, half], axis=0))
                out += LANES
            src += LANES
    assert out == W1_WIDTH and IN_WIDTH - src == GATE_WIDTH
    for j in range(GATE_WIDTH // LANES):
        emit(wg_ref, j * LANES, rows(src + j * LANES, LANES))


def _repack_call(w_in):
    depth, d, width = w_in.shape
    assert width == IN_WIDTH
    tr = 256
    w_t = jnp.swapaxes(w_in, 1, 2)
    return pl.pallas_call(
        _repack_kernel,
        name="repack_w_in",
        grid=(depth, d // tr),
        in_specs=[pl.BlockSpec((None, width, tr), lambda l, i: (l, 0, i))],
        out_specs=[pl.BlockSpec((None, tr, W1_WIDTH), lambda l, i: (l, i, 0)),
                   pl.BlockSpec((None, tr, GATE_WIDTH), lambda l, i: (l, i, 0))],
        out_shape=[jax.ShapeDtypeStruct((depth, d, W1_WIDTH), bf16),
                   jax.ShapeDtypeStruct((depth, d, GATE_WIDTH), bf16)],
        compiler_params=pltpu.CompilerParams(vmem_limit_bytes=VMEM_LIMIT),
    )(w_t)


def _proj_kernel(x_ref, mod_ref, c64_ref, s64_ref, cb_ref, sb_ref, w1_ref, qn_ref, kvn_ref, wuq_ref, wk_ref,
                 wv_ref, gq_ref, gk_ref,
                 qa_ref, ka_ref, va_ref, qb_ref, kb_ref, vb_ref, qc_ref, kc_ref, vc_ref, qd_ref, kd_ref, vd_ref,
                 *, scale_b):
    tm = x_ref.shape[0]
    h = _modulate(x_ref[...], mod_ref[0:1, :], mod_ref[1:2, :]).astype(bf16)
    lane = lax.broadcasted_iota(jnp.int32, (tm, LANES), 1)
    even = (lane & 1) == 0
    lo = lane < HALF
    c64, s64, cb, sb = c64_ref[...], s64_ref[...], cb_ref[...], sb_ref[...]

    def rope(v, cos, sin):
        partner = jnp.where(even, pltpu.roll(v, LANES - 1, 1), pltpu.roll(v, 1, 1))
        return v * cos + partner * sin

    def head_norm(v, gain):
        sq = v * v
        s_lo = jnp.sum(jnp.where(lo, sq, 0.0), axis=-1, keepdims=True)
        s_hi = jnp.sum(jnp.where(lo, 0.0, sq), axis=-1, keepdims=True)
        inv = jnp.where(lo, lax.rsqrt(s_lo * (1.0 / HALF) + EPS), lax.rsqrt(s_hi * (1.0 / HALF) + EPS))
        return v * inv * gain

    def chunk(z, j):
        return z[:, j * LANES:(j + 1) * LANES]

    qk_scale = DIFF_QK_DIM ** -0.5 * LOG2E

    z = _dot(h, w1_ref[:, 0:1536])
    for j in range(4):
        qa_ref[j] = (rope(chunk(z, j), c64, s64) * qk_scale).astype(bf16)
        ka_ref[j] = rope(chunk(z, 4 + j), c64, s64).astype(bf16)
        va_ref[j] = chunk(z, 8 + j).astype(bf16)

    z = _dot(h, w1_ref[:, 1536:2176])
    dq = (_rms(z[:, 0:256]) * qn_ref[...]).astype(bf16)
    ckv = (_rms(z[:, 256:512]) * kvn_ref[...]).astype(bf16)
    k_rot = rope(z[:, 512:640], cb, sb)
    zq = _dot(dq, wuq_ref[...])
    zk = _dot(ckv, wk_ref[...])
    zv = _dot(ckv, wv_ref[...])
    for j in range(MLA_HEADS):
        qb_ref[j] = (rope(chunk(zq, j), cb, sb) * scale_b).astype(bf16)
        kb_ref[j] = (chunk(zk, j) + k_rot).astype(bf16)
    for j in range(4):
        vb_ref[j] = chunk(zv, j).astype(bf16)

    z = _dot(h, w1_ref[:, 2176:3200])
    gq, gk = gq_ref[...], gk_ref[...]
    for j in range(4):
        qc_ref[j] = (rope(head_norm(chunk(z, j), gq), c64, s64) * qk_scale).astype(bf16)
    for j in range(2):
        kc_ref[j] = rope(head_norm(chunk(z, 4 + j), gk), c64, s64).astype(bf16)
        vc_ref[j] = chunk(z, 6 + j).astype(bf16)

    z = _dot(h, w1_ref[:, 3200:4224])
    for j in range(4):
        qd_ref[j] = (rope(chunk(z, j), c64, s64) * qk_scale).astype(bf16)
    for j in range(2):
        kd_ref[j] = rope(chunk(z, 4 + j), c64, s64).astype(bf16)
        vd_ref[j] = chunk(z, 6 + j).astype(bf16)


def _proj_call(xs, mod, tables, w, tm):
    b, t, d = xs.shape
    c64, s64, cb, sb = tables
    grid = (b, t // tm)

    def out(nchunks):
        return (jax.ShapeDtypeStruct((b, nchunks, t, LANES), bf16),
                pl.BlockSpec((None, nchunks, tm, LANES), lambda bi, i: (bi, 0, i, 0)))

    outs = [out(n) for n in (4, 4, 4, 8, 8, 4, 4, 2, 2, 4, 2, 2)]
    tab_spec = pl.BlockSpec((tm, LANES), lambda bi, i: (i, 0))
    scale_b = (MLA_NOPE_DIM + MLA_ROPE_DIM) ** -0.5 * LOG2E
    return pl.pallas_call(
        functools.partial(_proj_kernel, scale_b=scale_b),
        name="in_proj",
        grid=grid,
        in_specs=[
            pl.BlockSpec((None, tm, d), lambda bi, i: (bi, i, 0)),
            pl.BlockSpec((None, N_MOD, d), lambda bi, i: (bi, 0, 0)),
            tab_spec, tab_spec, tab_spec, tab_spec,
            _layer_spec(w["w1"], w["l"]), _const_spec((1, 256)), _const_spec((1, 256)),
            _const_spec(w["wuq"].shape), _const_spec(w["wk"].shape), _const_spec(w["wv"].shape),
            _const_spec((1, LANES)), _const_spec((1, LANES)),
        ],
        out_specs=[o[1] for o in outs],
        out_shape=[o[0] for o in outs],
        compiler_params=pltpu.CompilerParams(vmem_limit_bytes=VMEM_LIMIT),
    )(xs, mod, c64, s64, cb, sb, w["w1"], w["qn"], w["kvn"], w["wuq"], w["wk"], w["wv"], w["gq"], w["gk"])


def _flash_kernel(*refs, mode, has_x, has_sink, tk, lam_init):
    refs = list(refs)
    q_ref, kc_ref, vc_ref = refs[:3]
    pos = 3
    if has_x:
        k_ref, v_ref = refs[pos:pos + 2]
        pos += 2
    if mode == "A":
        dl_ref, subln_ref = refs[pos:pos + 2]
        pos += 2
    if has_sink:
        sink_ref = refs[pos]
        pos += 1
    o_ref, q2_ref, m_ref, acc_ref = refs[pos:pos + 4]
    s_refs, p_refs, al_refs = refs[pos + 4:pos + 6], refs[pos + 6:pos + 8], refs[pos + 8:pos + 10]

    tq = q_ref.shape[1]
    lane = lax.broadcasted_iota(jnp.int32, (tq, LANES), 1)
    lo = lane < HALF
    shared_k = mode != "B"
    if shared_k:
        q = q_ref[0]
        zero = jnp.zeros_like(q)
        q2_ref[0:tq, :] = jnp.where(lo, q, zero)
        q2_ref[tq:2 * tq, :] = jnp.where(lo, zero, q)
    else:
        q2_ref[0:tq, :] = q_ref[0]
        q2_ref[tq:2 * tq, :] = q_ref[1]

    for u in range(2):
        if has_sink:
            sink = sink_ref[2 * pl.program_id(1) + u] * LOG2E
            m_ref[u] = jnp.full((tq, LANES), sink, f32)
            acc_ref[u, :, LANES:] = jnp.ones((tq, LANES), f32)
        else:
            m_ref[u] = jnp.full((tq, LANES), NEG_INF, f32)
            acc_ref[u, :, LANES:] = jnp.zeros((tq, LANES), f32)
        acc_ref[u, :, :LANES] = jnp.zeros((tq, LANES), f32)

    def scores(chunk, s_ref):
        kref, _, start, size = chunk
        if shared_k:
            s_ref[:, :size] = _dot_nt(q2_ref[...], kref[0, start:start + size, :])
        else:
            for u in range(2):
                rows = slice(u * tq, (u + 1) * tq)
                s_ref[rows, :size] = _dot_nt(q2_ref[rows, :], kref[u, start:start + size, :])

    def softmax(chunk, s_ref, p_ref, al_ref):
        size = chunk[3]
        for u in range(2):
            rows = slice(u * tq, (u + 1) * tq)
            cols = [s_ref[rows, j * LANES:(j + 1) * LANES] for j in range(size // LANES)]
            m_prev = m_ref[u]
            m_new = jnp.maximum(m_prev, jnp.max(functools.reduce(jnp.maximum, cols), axis=-1, keepdims=True))
            al_ref[u] = jnp.exp2(m_prev - m_new)
            for j, cj in enumerate(cols):
                p_ref[rows, j * LANES:(j + 1) * LANES] = jnp.exp2(cj - m_new).astype(bf16)
            m_ref[u] = m_new

    def pv(chunk, p_ref, al_ref):
        _, vref, start, size = chunk
        vext = jnp.concatenate([vref[0, start:start + size, :], jnp.ones((size, LANES), bf16)], axis=1)
        for u in range(2):
            alpha = al_ref[u]
            alpha2 = jnp.concatenate([alpha, alpha], axis=1)
            acc_ref[u] = alpha2 * acc_ref[u] + _dot(p_ref[u * tq:(u + 1) * tq, :size], vext)

    chunks = [(kc_ref, vc_ref, 0, kc_ref.shape[1])]
    if has_x:
        chunks += [(k_ref, v_ref, c * tk, tk) for c in range(k_ref.shape[1] // tk)]

    for i in range(len(chunks) + 2):
        if i < len(chunks):
            scores(chunks[i], s_refs[i % 2])
        if 1 <= i <= len(chunks):
            softmax(chunks[i - 1], s_refs[(i - 1) % 2], p_refs[(i - 1) % 2], al_refs[(i - 1) % 2])
        if i >= 2:
            pv(chunks[i - 2], p_refs[i % 2], al_refs[i % 2])

    o0 = acc_ref[0, :, :LANES] / acc_ref[0, :, LANES:]
    o1 = acc_ref[1, :, :LANES] / acc_ref[1, :, LANES:]
    if mode == "A":
        dl = dl_ref[...]
        lam = (jnp.exp(jnp.sum(dl[0:1] * dl[1:2], axis=-1, keepdims=True))
               - jnp.exp(jnp.sum(dl[2:3] * dl[3:4], axis=-1, keepdims=True)) + lam_init)
        o = _rms(o0 - lam * o1) * subln_ref[...] * (1.0 - lam_init)
    else:
        o = jnp.where(lo, o0, o1)
    o_ref[...] = o.astype(bf16)


def _flash_call(mode, q, kc, vc, k, v, *, tq, tk, extras=(), sink=None, lam_init=0.0):
    b, _, t, _ = q.shape
    l_ctx = kc.shape[2]
    has_x = k is not None
    gq = 2 if mode == "B" else 1
    shared_kv = mode == "C"

    def kv_idx(bi, g, i):
        return (bi, g // 2 if shared_kv else g, 0, 0)

    in_specs = [
        pl.BlockSpec((None, gq, tq, LANES), lambda bi, g, i: (bi, g, i, 0)),
        pl.BlockSpec((None, gq, l_ctx, LANES), kv_idx),
        pl.BlockSpec((None, 1, l_ctx, LANES), kv_idx),
    ]
    args = [q, kc, vc]
    if has_x:
        n = k.shape[2]
        in_specs += [pl.BlockSpec((None, gq, n, LANES), kv_idx), pl.BlockSpec((None, 1, n, LANES), kv_idx)]
        args += [k, v]
    for e in extras:
        in_specs.append(pl.BlockSpec(e.shape, lambda bi, g, i, nd=e.ndim: (0,) * nd))
        args.append(e)
    if sink is not None:
        in_specs.append(pl.BlockSpec(memory_space=pltpu.SMEM))
        args.append(sink)
    state = pltpu.VMEM((2, tq, LANES), f32)
    width = max(tk, l_ctx) if has_x else l_ctx
    if has_x:
        assert k.shape[2] % tk == 0 and tk % LANES == 0
    s_buf, p_buf = pltpu.VMEM((2 * tq, width), f32), pltpu.VMEM((2 * tq, width), bf16)
    scratch = [pltpu.VMEM((2 * tq, LANES), bf16), state, pltpu.VMEM((2, tq, 2 * LANES), f32),
               s_buf, s_buf, p_buf, p_buf, state, state]
    return pl.pallas_call(
        functools.partial(_flash_kernel, mode=mode, has_x=has_x, has_sink=sink is not None, tk=tk,
                          lam_init=lam_init),
        name="flash_" + mode + ("" if has_x else "_ctx"),
        grid=(b, 4, t // tq),
        in_specs=in_specs,
        out_specs=pl.BlockSpec((None, tq, LANES), lambda bi, g, i: (bi, i, g)),
        out_shape=jax.ShapeDtypeStruct((b, t, 4 * LANES), bf16),
        scratch_shapes=scratch,
        compiler_params=pltpu.CompilerParams(vmem_limit_bytes=VMEM_LIMIT),
    )(*args)


def _window_kernel(q_ref, kc_ref, vc_ref, k_ref, v_ref, sink_ref, o_ref):
    tq = q_ref.shape[1]
    n = k_ref.shape[1]
    l_ctx = kc_ref.shape[1]
    band = tq + 2 * WINDOW
    width = l_ctx + band
    t0 = pl.program_id(1) * tq
    start = pl.multiple_of(jnp.clip(t0 - WINDOW, 0, n - band), WINDOW)
    col = lax.broadcasted_iota(jnp.int32, (tq, width), 1)
    qpos = t0 + lax.broadcasted_iota(jnp.int32, (tq, width), 0)
    valid = (col < l_ctx) | (jnp.abs(col - l_ctx + start - qpos) <= WINDOW)
    lane = lax.broadcasted_iota(jnp.int32, (tq, LANES), 1)
    lo = lane < HALF
    for j in range(4):
        if j % 2 == 0:
            kv = j // 2
            k_all = jnp.concatenate([kc_ref[kv], k_ref[kv, pl.ds(start, band), :]], axis=0)
            v_all = jnp.concatenate([vc_ref[kv], v_ref[kv, pl.ds(start, band), :]], axis=0)
            v_ext = jnp.concatenate([v_all, jnp.ones_like(v_all)], axis=1)
        q = q_ref[j]
        zero = jnp.zeros_like(q)
        s = _dot_nt(jnp.concatenate([jnp.where(lo, q, zero), jnp.where(lo, zero, q)], axis=0), k_all)
        outs = []
        for u in range(2):
            sink = sink_ref[2 * j + u] * LOG2E
            su = jnp.where(valid, s[u * tq:(u + 1) * tq], NEG_INF)
            cols = [su[:, c * LANES:(c + 1) * LANES] for c in range(width // LANES)]
            m = jnp.maximum(jnp.max(functools.reduce(jnp.maximum, cols), axis=-1, keepdims=True), sink)
            p = jnp.concatenate([jnp.exp2(c - m).astype(bf16) for c in cols], axis=1)
            acc = _dot(p, v_ext)
            outs.append(acc[:, :LANES] / (acc[:, LANES:] + jnp.exp2(sink - m)))
        o_ref[:, j * LANES:(j + 1) * LANES] = jnp.where(lo, outs[0], outs[1]).astype(bf16)


def _window_call(q, kc, vc, k, v, sink, *, tq):
    b, _, n, _ = q.shape
    l_ctx = kc.shape[2]
    return pl.pallas_call(
        _window_kernel,
        name="window_attn",
        grid=(b, n // tq),
        in_specs=[
            pl.BlockSpec((None, 4, tq, LANES), lambda bi, i: (bi, 0, i, 0)),
            pl.BlockSpec((None, 2, l_ctx, LANES), lambda bi, i: (bi, 0, 0, 0)),
            pl.BlockSpec((None, 2, l_ctx, LANES), lambda bi, i: (bi, 0, 0, 0)),
            pl.BlockSpec((None, 2, n, LANES), lambda bi, i: (bi, 0, 0, 0)),
            pl.BlockSpec((None, 2, n, LANES), lambda bi, i: (bi, 0, 0, 0)),
            pl.BlockSpec(memory_space=pltpu.SMEM),
        ],
        out_specs=pl.BlockSpec((None, tq, 4 * LANES), lambda bi, i: (bi, i, 0)),
        out_shape=jax.ShapeDtypeStruct((b, n, 4 * LANES), bf16),
        compiler_params=pltpu.CompilerParams(vmem_limit_bytes=VMEM_LIMIT),
    )(q, kc, vc, k, v, sink)


def _merge_kernel(x_ref, mod_ref, oa_ref, ob_ref, oc_ref, od_ref, wg_ref, wbr_ref, wout_ref, o_ref):
    x = x_ref[...]
    h = _modulate(x, mod_ref[0:1, :], mod_ref[1:2, :]).astype(bf16)
    mix = None
    for k, br_ref in enumerate((oa_ref, ob_ref, oc_ref, od_ref)):
        g = _dot(h, wg_ref[:, k * D_MODEL:(k + 1) * D_MODEL])
        y = _dot(br_ref[...], wbr_ref[k])
        term = _sigmoid(g) * y
        mix = term if mix is None else mix + term
    o_ref[...] = x + mod_ref[2:3, :] * _dot(mix.astype(bf16), wout_ref[...])


def _merge_call(xs, mod, branches, w, tm):
    b, t, d = xs.shape
    tiles = t // tm
    x2 = xs.reshape(b * t, d)
    br = [o.reshape(b * t, BRANCH_WIDTH) for o in branches]
    br_spec = pl.BlockSpec((tm, BRANCH_WIDTH), lambda i: (i, 0))
    out = pl.pallas_call(
        _merge_kernel,
        name="merge",
        grid=(b * tiles,),
        in_specs=[
            pl.BlockSpec((tm, d), lambda i: (i, 0)),
            pl.BlockSpec((None, N_MOD, d), lambda i: (i // tiles, 0, 0)),
            br_spec, br_spec, br_spec, br_spec,
            _layer_spec(w["wg"], w["l"]), _layer_spec(w["wbr"], w["l"]), _layer_spec(w["wout"], w["l"]),
        ],
        out_specs=pl.BlockSpec((tm, d), lambda i: (i, 0)),
        out_shape=jax.ShapeDtypeStruct((b * t, d), f32),
        compiler_params=pltpu.CompilerParams(vmem_limit_bytes=VMEM_LIMIT),
    )(x2, mod, *br, w["wg"], w["wbr"], w["wout"])
    return out.reshape(b, t, d)


def _ffn_kernel(x_ref, xp_ref, xn_ref, mod_ref, wup_ref, cw_ref, cb_ref, wdn_ref, fin_ref, o_ref, u_ref, *,
                seq_len, final_norm):
    tm = x_ref.shape[0]
    x = x_ref[...]
    shift, scale = mod_ref[3:4, :], mod_ref[4:5, :]
    tile = pl.program_id(0) % (seq_len // tm)
    keep_prev = (tile != 0).astype(f32)
    keep_next = (tile != seq_len // tm - 1).astype(f32)
    h = jnp.concatenate([
        (_modulate(xp_ref[...], shift, scale) * keep_prev).astype(bf16),
        _modulate(x, shift, scale).astype(bf16),
        (_modulate(xn_ref[...], shift, scale) * keep_next).astype(bf16)], axis=0)

    def conv(c0, width):
        cols = slice(c0, c0 + width)
        u_ref[:, cols] = _dot(h, wup_ref[:, cols])
        prev = u_ref[HALO - 1:HALO - 1 + tm, cols]
        cur = u_ref[HALO:HALO + tm, cols]
        nxt = u_ref[HALO + 1:HALO + 1 + tm, cols]
        return prev * cw_ref[0:1, cols] + cur * cw_ref[1:2, cols] + nxt * cw_ref[2:3, cols] + cb_ref[:, cols]

    acc = None
    cv = 0
    for width in FFN_CHUNKS:
        val = conv(cv, width)
        gate = conv(D_FF + cv, width)
        a = (gate * _sigmoid(gate) * val).astype(bf16)
        part = _dot(a, wdn_ref[cv:cv + width, :])
        acc = part if acc is None else acc + part
        cv += width
    out = x + mod_ref[5:6, :] * acc
    if final_norm:
        out = _rms(out) * fin_ref[...]
    o_ref[...] = out


def _ffn_call(xs, mod, w, fin, tm, final_norm):
    b, t, d = xs.shape
    tiles = t // tm
    x2 = xs.reshape(b * t, d)
    hb = tm // HALO
    last_halo = b * t // HALO - 1
    out = pl.pallas_call(
        functools.partial(_ffn_kernel, seq_len=t, final_norm=final_norm),
        name="conv_ffn",
        grid=(b * tiles,),
        in_specs=[
            pl.BlockSpec((tm, d), lambda i: (i, 0)),
            pl.BlockSpec((HALO, d), lambda i: (jnp.maximum(i * hb - 1, 0), 0)),
            pl.BlockSpec((HALO, d), lambda i: (jnp.minimum((i + 1) * hb, last_halo), 0)),
            pl.BlockSpec((None, N_MOD, d), lambda i: (i // tiles, 0, 0)),
            _layer_spec(w["wup"], w["l"]), _layer_spec(w["cw"], w["l"]), _layer_spec(w["cb"], w["l"]),
            _layer_spec(w["wdn"], w["l"]), _const_spec((1, d)),
        ],
        out_specs=pl.BlockSpec((tm, d), lambda i: (i, 0)),
        out_shape=jax.ShapeDtypeStruct((b * t, d), f32),
        scratch_shapes=[pltpu.VMEM((tm + 2 * HALO, 2 * D_FF), f32)],
        compiler_params=pltpu.CompilerParams(vmem_limit_bytes=VMEM_LIMIT),
    )(x2, x2, x2, mod, w["wup"], w["cw"], w["cb"], w["wdn"], fin)
    return out.reshape(b, t, d)


def _layer_weights(l, stacked, mla_q_norm, mla_kv_norm, mla_w_uq, mla_w_ukv, gqa_q_norm, gqa_k_norm):
    hq = MLA_NOPE_DIM + MLA_ROPE_DIM
    wuq = mla_w_uq[l].reshape(MLA_Q_LORA, MLA_HEADS, hq)
    wuq = jnp.pad(wuq, ((0, 0), (0, 0), (0, LANES - hq))).reshape(MLA_Q_LORA, MLA_HEADS * LANES).astype(bf16)
    wukv = mla_w_ukv[l].reshape(MLA_KV_LORA, MLA_HEADS, MLA_NOPE_DIM + MLA_V_DIM)
    wk = jnp.pad(wukv[:, :, :MLA_NOPE_DIM], ((0, 0), (0, 0), (0, LANES - MLA_NOPE_DIM)))
    wk = wk.reshape(MLA_KV_LORA, MLA_HEADS * LANES).astype(bf16)
    wv = wukv[:, :, MLA_NOPE_DIM:].reshape(MLA_KV_LORA, MLA_HEADS * MLA_V_DIM).astype(bf16)
    return dict(
        stacked, l=l, wuq=wuq, wk=wk, wv=wv,
        qn=mla_q_norm[l].reshape(1, -1), kvn=mla_kv_norm[l].reshape(1, -1),
        gq=jnp.tile(gqa_q_norm[l], 2).reshape(1, LANES), gk=jnp.tile(gqa_k_norm[l], 2).reshape(1, LANES))


def _rope_tables(n, l_ctx):
    t = jnp.arange(n, dtype=jnp.int32)
    row, col = (t // GRID_W).astype(f32), (t % GRID_W).astype(f32)

    def expanded(rot_dim):
        axis_dim = rot_dim // 2
        inv = ROPE_THETA ** (-jnp.arange(0, axis_dim, 2, dtype=f32) / axis_dim)
        ang = jnp.concatenate([row[:, None] * inv, col[:, None] * inv], axis=-1)
        cos = jnp.repeat(jnp.cos(ang), 2, axis=-1)
        sin = jnp.repeat(jnp.sin(ang), 2, axis=-1) * jnp.tile(jnp.array([-1.0, 1.0], f32), rot_dim // 2)
        return cos, sin

    c64, s64 = expanded(2 * HALF // 2)
    c64, s64 = jnp.tile(c64, (1, 2)), jnp.tile(s64, (1, 2))
    c32, s32 = expanded(MLA_ROPE_DIM)
    pad = LANES - MLA_NOPE_DIM - MLA_ROPE_DIM
    cb = jnp.concatenate([jnp.ones((n, MLA_NOPE_DIM), f32), c32, jnp.ones((n, pad), f32)], axis=1)
    sb = jnp.concatenate([jnp.zeros((n, MLA_NOPE_DIM), f32), s32, jnp.zeros((n, pad), f32)], axis=1)
    ones, zeros = jnp.ones((l_ctx, LANES), f32), jnp.zeros((l_ctx, LANES), f32)
    return (c64, s64, cb, sb), (ones, zeros, ones, zeros)


def kernel(x, c, ctx, c_ctx, w_mod, b_mod, w_in, diff_lambda, diff_subln, mla_q_norm, mla_kv_norm, mla_w_uq,
           mla_w_ukv, gqa_q_norm, gqa_k_norm, swa_sink, w_branch, w_out, ffn_w_up, ffn_conv_w, ffn_conv_b,
           ffn_w_down, final_norm):
    b, n, d = x.shape
    l_ctx = ctx.shape[1]
    depth = w_mod.shape[0]
    tm_x = min(512, n)
    tq = min(512, n)
    tq_b = min(256, n)
    tk = min(1024, n)
    tk_a = min(2048, n)

    cs = jnp.zeros((8, d), f32).at[:b].set(c).at[b].set(c_ctx)
    mods = _mod_call(cs, w_mod, b_mod).reshape(depth, 8, N_MOD, d)
    tab_x, tab_c = _rope_tables(n, l_ctx)
    fin = final_norm.reshape(1, d)

    w1, wg = _repack_call(w_in)
    stacked = dict(w1=w1, wg=wg, wbr=w_branch.astype(bf16), wout=w_out.astype(bf16),
                   wup=ffn_w_up.astype(bf16), wdn=ffn_w_down.astype(bf16), cw=ffn_conv_w,
                   cb=ffn_conv_b.reshape(depth, 1, -1))

    xs, xc = x, ctx
    for l in range(depth):
        w = _layer_weights(l, stacked, mla_q_norm, mla_kv_norm, mla_w_uq, mla_w_ukv, gqa_q_norm, gqa_k_norm)
        mod_x = mods[l, :b]
        mod_c = jnp.broadcast_to(mods[l, b][None], (b, N_MOD, d))
        lam_init = 0.8 - 0.6 * math.exp(-0.3 * l)
        a_extras = (diff_lambda[l], diff_subln[l].reshape(1, LANES))
        sink = swa_sink[l]
        update_ctx = l < depth - 1

        px = _proj_call(xs, mod_x, tab_x, w, tm_x)
        pc = _proj_call(xc, mod_c, tab_c, w, l_ctx)
        qa, ka, va, qb, kb, vb, qc, kc, vc, qd, kd, vd = px
        qa_c, ka_c, va_c, qb_c, kb_c, vb_c, qc_c, kc_c, vc_c, qd_c, kd_c, vd_c = pc

        oa = _flash_call("A", qa, ka_c, va_c, ka, va, tq=tq, tk=tk_a, extras=a_extras, lam_init=lam_init)
        ob = _flash_call("B", qb, kb_c, vb_c, kb, vb, tq=tq_b, tk=tk)
        oc = _flash_call("C", qc, kc_c, vc_c, kc, vc, tq=tq, tk=tk)
        od = _window_call(qd, kd_c, vd_c, kd, vd, sink, tq=min(256, n))
        xs_mid = _merge_call(xs, mod_x, (oa, ob, oc, od), w, tm_x)
        xs_new = _ffn_call(xs_mid, mod_x, w, fin, tm_x, final_norm=not update_ctx)

        if update_ctx:
            oa = _flash_call("A", qa_c, ka_c, va_c, None, None, tq=l_ctx, tk=tk, extras=a_extras,
                             lam_init=lam_init)
            ob = _flash_call("B", qb_c, kb_c, vb_c, None, None, tq=l_ctx, tk=tk)
            oc = _flash_call("C", qc_c, kc_c, vc_c, None, None, tq=l_ctx, tk=tk)
            od = _flash_call("C", qd_c, kd_c, vd_c, None, None, tq=l_ctx, tk=tk, sink=sink)
            xc_mid = _merge_call(xc, mod_c, (oa, ob, oc, od), w, l_ctx)
            xc = _ffn_call(xc_mid, mod_c, w, fin, l_ctx, final_norm=False)
        xs = xs_new
    return xs
```
